```python
import jax
import jax.numpy as jnp
from jax import lax
import numpy as np

D_MODEL = 1024
BATCH = 4
SEQ = 8192
DEPTH = 2

GRID_W = 64
CTX_LEN = 256
N_MIXERS = 2
EPS = 1e-6
NA_HEADS = 16
NA_HEAD_DIM = D_MODEL // NA_HEADS
WIN_H = 8
WIN_W = 16
HG_HEADS = 8
HG_EXPAND = 128
HG_FDIM = HG_HEADS * HG_EXPAND
HG_HEAD_V = D_MODEL // HG_HEADS
CHUNK = 64
D_FF = -(-(8 * D_MODEL) // (3 * 256)) * 256
N_NA_LAYERS = (DEPTH + 1) // 2
N_HG_LAYERS = DEPTH // 2

kernel_name = 'hybrid_na_hgrn2_dit_block'


def rms_norm(x, g):
    xf = x.astype(jnp.float32)
    y = xf * lax.rsqrt(jnp.mean(xf * xf, axis=-1, keepdims=True) + EPS)
    return (y * g.astype(jnp.float32)).astype(x.dtype)


def ada_mod(cvec, w, b):
    return jnp.split(jax.nn.silu(cvec) @ w + b, 6, axis=-1)


def swiglu(h, w_in, w_out):
    a, u = jnp.split(h @ w_in, 2, axis=-1)
    return (jax.nn.silu(a) * u) @ w_out


def na_mixer(h_lat, h_ctx, w_qkv, w_o, q_gain, k_gain, rpb, need_ctx_out):
    B, T, _ = h_lat.shape
    rows = T // GRID_W
    kh = min(WIN_H, rows)
    scale = NA_HEAD_DIM ** -0.5

    def qkv(h):
        n = h.shape[1]
        q, k, v = jnp.split(h @ w_qkv, 3, axis=-1)
        q = rms_norm(q.reshape(B, n, NA_HEADS, NA_HEAD_DIM), q_gain) * scale
        k = rms_norm(k.reshape(B, n, NA_HEADS, NA_HEAD_DIM), k_gain)
        return q, k, v.reshape(B, n, NA_HEADS, NA_HEAD_DIM)

    q_c, k_c, v_c = qkv(h_ctx)
    q_l, k_l, v_l = qkv(h_lat)

    out_ctx = None
    if need_ctx_out:
        s = jnp.einsum('bqhd,bkhd->bhqk', q_c, k_c)
        p = jax.nn.softmax(s.astype(jnp.float32), axis=-1).astype(v_c.dtype)
        o_c = jnp.einsum('bhqk,bkhd->bqhd', p, v_c)
        out_ctx = o_c.reshape(B, h_ctx.shape[1], D_MODEL) @ w_o

    grid = lambda a: a.reshape(B, rows, GRID_W, NA_HEADS, NA_HEAD_DIM)
    q_g, k_g, v_g = grid(q_l), grid(k_l), grid(v_l)
    row_start = jnp.clip(jnp.arange(rows) - kh // 2, 0, rows - kh)
    cols = jnp.arange(GRID_W)
    col_idx = jnp.clip(cols - WIN_W // 2, 0, GRID_W - WIN_W)[:, None] + jnp.arange(WIN_W)
    rpb_x = rpb[:, :, col_idx - cols[:, None] + (WIN_W - 1)]
    n_loc = kh * WIN_W

    def row_block(r):
        rs = row_start[r]
        q_row = lax.dynamic_index_in_dim(q_g, r, axis=1, keepdims=False)
        k_win = lax.dynamic_slice_in_dim(k_g, rs, kh, axis=1)[:, :, col_idx]
        v_win = lax.dynamic_slice_in_dim(v_g, rs, kh, axis=1)[:, :, col_idx]
        bias = rpb_x[:, rs + jnp.arange(kh) - r + (WIN_H - 1)]
        s_loc = jnp.einsum('bqhd,bjqwhd->bhqjw', q_row, k_win) + bias.transpose(0, 2, 1, 3)[None]
        s_ctx = jnp.einsum('bqhd,bkhd->bhqk', q_row, k_c)
        s = jnp.concatenate([s_loc.reshape(B, NA_HEADS, GRID_W, n_loc), s_ctx], axis=-1)
        p = jax.nn.softmax(s.astype(jnp.float32), axis=-1).astype(v_win.dtype)
        p_loc = p[..., :n_loc].reshape(B, NA_HEADS, GRID_W, kh, WIN_W)
        return (jnp.einsum('bhqjw,bjqwhd->bqhd', p_loc, v_win)
                + jnp.einsum('bhqk,bkhd->bqhd', p[..., n_loc:], v_c))

    o = lax.map(row_block, jnp.arange(rows))
    out_lat = o.transpose(1, 0, 2, 3, 4).reshape(B, T, D_MODEL) @ w_o
    return out_lat, out_ctx


def forget_gate(f_pre, lb):
    f32 = f_pre.astype(jnp.float32)
    log_f = jnp.logaddexp(jnp.log(lb), jnp.log1p(-lb) + jax.nn.log_sigmoid(f32))
    k = (1.0 - lb) * jax.nn.sigmoid(-f32)
    return k.astype(f_pre.dtype), log_f


def chunked_gated_scan(q, k, v, log_f, s0):
    B, N, H, _ = q.shape
    dv = v.shape[-1]
    nc = N // CHUNK
    chunks = lambda a: a.reshape(B, nc, CHUNK, H, a.shape[-1]).transpose(1, 0, 3, 2, 4)
    lower = jnp.tril(jnp.ones((CHUNK, CHUNK), dtype=bool))[:, :, None]

    def step(S, xs):
        qc, kc, vc, gc = xs
        b = jnp.cumsum(gc, axis=2)
        decay = jnp.exp(jnp.where(lower, b[:, :, :, None, :] - b[:, :, None, :, :], -jnp.inf))
        a = jnp.einsum('bhtk,bhtsk,bhsk->bhts', qc, decay, kc)
        o = jnp.einsum('bhts,bhsv->bhtv', a, vc) + jnp.einsum('bhtk,bhkv->bhtv', qc * jnp.exp(b), S)
        b_end = b[:, :, -1]
        S_new = (jnp.exp(b_end)[..., None] * S
                 + jnp.einsum('bhsk,bhsv->bhkv', kc * jnp.exp(b_end[:, :, None] - b), vc))
        return S_new, o

    s_fin, o = lax.scan(step, s0, (chunks(q), chunks(k), chunks(v), chunks(log_f)))
    return o.transpose(1, 0, 3, 2, 4).reshape(B, N, H, dv).astype(v.dtype), s_fin


def hgrn2_mixer(h_lat, h_ctx, w_in, lb, norm_g, w_o, need_ctx_out):
    B = h_lat.shape[0]
    splits = [HG_FDIM, HG_FDIM + D_MODEL, HG_FDIM + 2 * D_MODEL, 2 * HG_FDIM + 2 * D_MODEL]

    def project(h):
        n = h.shape[1]
        q, v, g, f_fwd, f_bwd = jnp.split(h @ w_in, splits, axis=-1)
        heads = lambda a: a.reshape(B, n, HG_HEADS, -1)
        return heads(jax.nn.silu(q)), heads(v), g, heads(f_fwd), heads(f_bwd)

    def bidirectional(q, v, f_fwd, f_bwd, s_fwd, s_bwd):
        k_f, lf_f = forget_gate(f_fwd, lb[0])
        k_b, lf_b = forget_gate(f_bwd, lb[1])
        o_f, s_fwd = chunked_gated_scan(q, k_f, v, lf_f, s_fwd)
        o_b, s_bwd = chunked_gated_scan(q[:, ::-1], k_b[:, ::-1], v[:, ::-1], lf_b[:, ::-1], s_bwd)
        return o_f + o_b[:, ::-1], s_fwd, s_bwd

    def readout(o, g):
        n = o.shape[1]
        return (rms_norm(o, norm_g).reshape(B, n, D_MODEL) * jax.nn.silu(g)) @ w_o

    zero = jnp.zeros((B, HG_HEADS, HG_EXPAND, HG_HEAD_V), jnp.float32)
    q_c, v_c, g_c, ff_c, fb_c = project(h_ctx)
    o_c, s_fwd, s_bwd = bidirectional(q_c, v_c, ff_c, fb_c, zero, zero)
    q_l, v_l, g_l, ff_l, fb_l = project(h_lat)
    o_l, _, _ = bidirectional(q_l, v_l, ff_l, fb_l, s_fwd, s_bwd)
    out_lat = readout(o_l, g_l)
    out_ctx = readout(o_c, g_c) if need_ctx_out else None
    return out_lat, out_ctx


def setup_inputs(seed: int = 0) -> dict:
    key = jax.random.key(seed)
    ks = jax.random.split(key, 20)
    f32 = jnp.float32

    def w(k, shape, fan_in):
        return jax.random.normal(k, shape, f32) * fan_in ** -0.5

    def gain(k, shape):
        return 1.0 + 0.02 * jax.random.normal(k, shape, f32)

    return {
        'x': jax.random.normal(ks[0], (BATCH, SEQ, D_MODEL), f32),
        'c': jax.random.normal(ks[1], (BATCH, D_MODEL), f32),
        'ctx': jax.random.normal(ks[2], (BATCH, CTX_LEN, D_MODEL), f32),
        'c_ctx': jax.random.normal(ks[3], (D_MODEL,), f32),
        'ada_w': w(ks[4], (DEPTH, D_MODEL, 6 * D_MODEL), D_MODEL),
        'ada_b': 0.02 * jax.random.normal(ks[5], (DEPTH, 6 * D_MODEL), f32),
        'norm1_g': gain(ks[6], (DEPTH, D_MODEL)),
        'norm2_g': gain(ks[7], (DEPTH, D_MODEL)),
        'na_w_qkv': w(ks[8], (N_NA_LAYERS, D_MODEL, 3 * D_MODEL), D_MODEL),
        'na_w_o': w(ks[9], (N_NA_LAYERS, D_MODEL, D_MODEL), D_MODEL),
        'na_q_gain': gain(ks[10], (N_NA_LAYERS, NA_HEAD_DIM)),
        'na_k_gain': gain(ks[11], (N_NA_LAYERS, NA_HEAD_DIM)),
        'na_rpb': 0.2 * jax.random.normal(ks[12], (N_NA_LAYERS, NA_HEADS, 2 * WIN_H - 1, 2 * WIN_W - 1), f32),
        'hg_w_in': w(ks[13], (N_HG_LAYERS, D_MODEL, 3 * HG_FDIM + 2 * D_MODEL), D_MODEL),
        'hg_lower': 0.5 * jax.random.normal(ks[14], (DEPTH, 2, HG_FDIM), f32),
        'hg_norm_g': gain(ks[15], (N_HG_LAYERS, HG_HEAD_V)),
        'hg_w_o': w(ks[16], (N_HG_LAYERS, D_MODEL, D_MODEL), D_MODEL),
        'ffn_w_in': w(ks[17], (DEPTH, D_MODEL, 2 * D_FF), D_MODEL),
        'ffn_w_out': w(ks[18], (DEPTH, D_FF, D_MODEL), D_FF),
    }


def reference(x, c, ctx, c_ctx, ada_w, ada_b, norm1_g, norm2_g, na_w_qkv, na_w_o, na_q_gain,
              na_k_gain, na_rpb, hg_w_in, hg_lower, hg_norm_g, hg_w_o, ffn_w_in, ffn_w_out):
    lbs = jnp.cumsum(jax.nn.softmax(hg_lower.astype(jnp.float32), axis=0), axis=0)
    lbs = (lbs - lbs[:1]).reshape(DEPTH, 2, HG_HEADS, HG_EXPAND)
    x_lat, x_ctx = x, ctx
    for i in range(DEPTH):
        last = i == DEPTH - 1
        j = i // N_MIXERS
        sh1, sc1, g1, sh2, sc2, g2 = [m[:, None, :] for m in ada_mod(c, ada_w[i], ada_b[i])]
        csh1, csc1, cg1, csh2, csc2, cg2 = ada_mod(c_ctx, ada_w[i], ada_b[i])
        h_lat = rms_norm(x_lat, norm1_g[i]) * (1 + sc1) + sh1
        h_ctx = rms_norm(x_ctx, norm1_g[i]) * (1 + csc1) + csh1
        if i % N_MIXERS == 0:
            out_lat, out_ctx = na_mixer(h_lat, h_ctx, na_w_qkv[j], na_w_o[j], na_q_gain[j],
                                        na_k_gain[j], na_rpb[j], not last)
        else:
            out_lat, out_ctx = hgrn2_mixer(h_lat, h_ctx, hg_w_in[j], lbs[i], hg_norm_g[j],
                                           hg_w_o[j], not last)
        x_lat = x_lat + g1 * out_lat
        x_lat = x_lat + g2 * swiglu(rms_norm(x_lat, norm2_g[i]) * (1 + sc2) + sh2, ffn_w_in[i], ffn_w_out[i])
        if not last:
            x_ctx = x_ctx + cg1 * out_ctx
            x_ctx = x_ctx + cg2 * swiglu(rms_norm(x_ctx, norm2_g[i]) * (1 + csc2) + csh2,
                                         ffn_w_in[i], ffn_w_out[i])
    return x_lat
```

```python
import functools

import jax
import jax.numpy as jnp
import numpy as np
from jax import lax
from jax.experimental import pallas as pl
from jax.experimental.pallas import tpu as pltpu

F32 = jnp.float32
BF16 = jnp.bfloat16

D_MODEL = 1024
EPS = 1e-6
GRID_W = 64
WIN_H = 8
WIN_W = 16
NA_HEADS = 16
NA_HEAD_DIM = D_MODEL // NA_HEADS
HG_HEADS = 8
HG_DK = 128
D_FF = 2816
N_MOD = 6

LANES = 128
Q_ROWS = 4
Q_BLK = Q_ROWS * GRID_W
KEY_SLOTS = 3
MASKED = -1e30
SCAN_CHUNK = 64
EXP_CLAMP = 80.0
VMEM_LIMIT = 56 * 1024 * 1024


def _dot(a, b):
    return jnp.dot(a, b, preferred_element_type=F32)


def _dot_nt(a, b):
    return lax.dot_general(a, b, (((1,), (1,)), ((), ())), preferred_element_type=F32)


def _dot_tn(a, b):
    return lax.dot_general(a, b, (((0,), (0,)), ((), ())), preferred_element_type=F32)


def _silu(x):
    return x * (1.0 / (1.0 + jnp.exp(-x)))


def _norm_mod(xf, g, sc, sh):
    ms = jnp.mean(xf * xf, axis=-1, keepdims=True)
    return (xf * lax.rsqrt(ms + EPS) * g) * (1.0 + sc) + sh


def _split_bf16(x):
    hi = x.astype(BF16)
    lo = (x - hi.astype(F32)).astype(BF16)
    return hi, lo


def _resident(shape):
    zeros = (0,) * len(shape)
    return pl.BlockSpec(shape, lambda *_: zeros, pipeline_mode=pl.Buffered(1))


def _mod_spec(arr):
    if arr.shape[0] == 1:
        return pl.BlockSpec((None, 1, D_MODEL), lambda b, i: (0, 0, 0))
    return pl.BlockSpec((None, 1, D_MODEL), lambda b, i: (b, 0, 0))


def _tok_spec(tm, width=D_MODEL):
    return pl.BlockSpec((None, tm, width), lambda b, i: (b, i, 0))


def _params(*sem):
    return pltpu.CompilerParams(dimension_semantics=sem, vmem_limit_bytes=VMEM_LIMIT)


def _ada_kernel(c_ref, w_ref, b_ref, o_ref):
    s = _silu(c_ref[...]).astype(BF16)
    o_ref[...] = _dot(s, w_ref[...].astype(BF16)) + b_ref[...]


def _ada(cvec, ada_w, ada_b):
    depth, _, n = ada_w.shape
    tn = 1536
    return pl.pallas_call(
        _ada_kernel,
        grid=(depth, n // tn),
        in_specs=[pl.BlockSpec((8, D_MODEL), lambda l, j: (0, 0)),
                  pl.BlockSpec((None, D_MODEL, tn), lambda l, j: (l, 0, j)),
                  pl.BlockSpec((None, 1, tn), lambda l, j: (l, 0, j))],
        out_specs=pl.BlockSpec((None, 8, tn), lambda l, j: (l, 0, j)),
        out_shape=jax.ShapeDtypeStruct((depth, 8, n), F32),
        compiler_params=_params("arbitrary", "arbitrary"),
        name="ada",
    )(cvec, ada_w, ada_b.reshape(depth, 1, n))


def _qkv_kernel(x_ref, g_ref, sc_ref, sh_ref, w_ref, gq_ref, gk_ref, red_ref, exp_ref,
                q_ref, k_ref, v_ref):
    h = _norm_mod(x_ref[...], g_ref[...], sc_ref[...], sh_ref[...]).astype(BF16)

    def head_norm(z, gain):
        hi, lo = _split_bf16(z * z)
        ss = _dot(hi, red_ref[...]) + _dot(lo, red_ref[...])
        r = lax.rsqrt(ss * (1.0 / NA_HEAD_DIM) + EPS)
        rhi, rlo = _split_bf16(r)
        rex = _dot(rhi, exp_ref[...]) + _dot(rlo, exp_ref[...])
        return z * rex * gain

    q = _dot(h, w_ref[:, 0:D_MODEL])
    q_ref[...] = head_norm(q, gq_ref[...]).astype(BF16)
    k = _dot(h, w_ref[:, D_MODEL:2 * D_MODEL])
    k_ref[...] = head_norm(k, gk_ref[...]).astype(BF16)
    v_ref[...] = _dot(h, w_ref[:, 2 * D_MODEL:3 * D_MODEL]).astype(BF16)


def _qkv(x, g, sc, sh, w, gq, gk, red, expand, tm):
    B, T, _ = x.shape
    out = jax.ShapeDtypeStruct((B, T, D_MODEL), BF16)
    return pl.pallas_call(
        _qkv_kernel,
        grid=(B, T // tm),
        in_specs=[_tok_spec(tm), _resident((1, D_MODEL)), _mod_spec(sc), _mod_spec(sh),
                  _resident(w.shape), _resident((1, D_MODEL)), _resident((1, D_MODEL)),
                  _resident(red.shape), _resident(expand.shape)],
        out_specs=[_tok_spec(tm)] * 3,
        out_shape=[out] * 3,
        compiler_params=_params("parallel", "parallel"),
        name="qkv",
    )(x, g, sc, sh, w, gq, gk, red, expand)


def _head_mask(width):
    return lax.broadcasted_iota(jnp.int32, (1, LANES), 1) < width


def _softmax_pv(q_pair, key_refs, val_refs, bias_of_head):
    first = _head_mask(NA_HEAD_DIM)
    outs = []
    for h in range(2):
        own = first if h == 0 else jnp.logical_not(first)
        qh = jnp.where(own, q_pair, jnp.zeros_like(q_pair))
        s = jnp.concatenate([_dot_nt(qh, kr[...]) for kr in key_refs], axis=1)
        bias = bias_of_head(h)
        if bias is not None:
            s = s + bias
        m = jnp.max(s, axis=-1, keepdims=True)
        p = jnp.exp(s - m)
        l = jnp.sum(p, axis=-1, keepdims=True)
        pb = p.astype(BF16)
        o = None
        off = 0
        for vr in val_refs:
            n = vr.shape[0]
            t = _dot(pb[:, off:off + n], vr[...])
            o = t if o is None else o + t
            off += n
        outs.append(o / l)
    return jnp.where(first, outs[0], outs[1])


def _attn_kernel(q_ref, k0_ref, k1_ref, k2_ref, v0_ref, v1_ref, v2_ref, kc_ref, vc_ref, bias_ref, o_ref):
    n_loc = KEY_SLOTS * Q_BLK
    n_ctx = kc_ref.shape[0]

    def bias_of_head(h):
        return jnp.concatenate([bias_ref[h], jnp.zeros((Q_BLK, n_ctx), F32)], axis=1)

    del n_loc
    o = _softmax_pv(q_ref[...], (k0_ref, k1_ref, k2_ref, kc_ref), (v0_ref, v1_ref, v2_ref, vc_ref),
                    bias_of_head)
    o_ref[...] = o.astype(BF16)


def _attn(q, k, v, kc, vc, bias):
    B, T, _ = q.shape
    L = kc.shape[1]
    nb = T // Q_BLK
    n_pairs = D_MODEL // LANES

    def variant(i):
        return jnp.where(i == 0, 0, jnp.where(i == nb - 1, 2, 1))

    def halo(d):
        return pl.BlockSpec((None, Q_BLK, LANES),
                            lambda p, b, i: (b, jnp.clip(i + d, 0, nb - 1), p))

    ctx_spec = pl.BlockSpec((None, L, LANES), lambda p, b, i: (b, 0, p))
    return pl.pallas_call(
        _attn_kernel,
        grid=(n_pairs, B, nb),
        in_specs=[pl.BlockSpec((None, Q_BLK, LANES), lambda p, b, i: (b, i, p)),
                  halo(-1), halo(0), halo(1), halo(-1), halo(0), halo(1),
                  ctx_spec, ctx_spec,
                  pl.BlockSpec((None, 2, Q_BLK, KEY_SLOTS * Q_BLK), lambda p, b, i: (variant(i), p, 0, 0))],
        out_specs=pl.BlockSpec((None, Q_BLK, LANES), lambda p, b, i: (b, i, p)),
        out_shape=jax.ShapeDtypeStruct((B, T, D_MODEL), BF16),
        compiler_params=_params("parallel", "parallel", "parallel"),
        name="attn",
    )(q, k, k, k, v, v, v, kc, vc, bias)


def _attn_ctx_kernel(q_ref, k_ref, v_ref, o_ref):
    o_ref[...] = _softmax_pv(q_ref[...], (k_ref,), (v_ref,), lambda h: None).astype(BF16)


def _attn_ctx(q, k, v):
    B, L, _ = q.shape
    spec = pl.BlockSpec((None, L, LANES), lambda b, p: (b, 0, p))
    return pl.pallas_call(
        _attn_ctx_kernel,
        grid=(B, D_MODEL // LANES),
        in_specs=[spec] * 3,
        out_specs=spec,
        out_shape=jax.ShapeDtypeStruct((B, L, D_MODEL), BF16),
        compiler_params=_params("parallel", "parallel"),
        name="attn_ctx",
    )(q, k, v)


def _attn_bias(rpb, rows):
    a = np.arange(Q_ROWS)[:, None, None, None]
    c = np.arange(GRID_W)[None, :, None, None]
    j = np.arange(KEY_SLOTS * Q_ROWS)[None, None, :, None]
    cc = np.arange(GRID_W)[None, None, None, :]
    shape = (Q_ROWS, GRID_W, KEY_SLOTS * Q_ROWS, GRID_W)
    d_row = j - Q_ROWS - a
    col_start = np.clip(c - WIN_W // 2, 0, GRID_W - WIN_W)
    col_ok = (cc >= col_start) & (cc < col_start + WIN_W)
    r_idx = np.broadcast_to(np.clip(d_row + WIN_H - 1, 0, 2 * WIN_H - 2), shape)
    c_idx = np.broadcast_to(np.clip(cc - c + WIN_W - 1, 0, 2 * WIN_W - 2), shape)
    vals = rpb[:, r_idx, c_idx]
    kh = min(WIN_H, rows)
    lows = (np.full_like(a, Q_ROWS), a + Q_ROWS - kh // 2, np.full_like(a, 2 * Q_ROWS - kh))
    out = []
    for lo in lows:
        ok = np.broadcast_to((j >= lo) & (j < lo + kh) & col_ok, shape)
        out.append(jnp.where(ok[None], vals, MASKED).reshape(rpb.shape[0], Q_BLK, KEY_SLOTS * Q_BLK))
    return jnp.stack(out)


FF_CHUNK = D_FF // 2


def _mix_ffn_tail(a, x_ref, g1_ref, n2_ref, sc2_ref, sh2_ref, g2_ref, wo_ref, win_ref, wout_ref, o_ref):
    x1 = x_ref[...] + g1_ref[...] * _dot(a, wo_ref[...])
    h2 = _norm_mod(x1, n2_ref[...], sc2_ref[...], sh2_ref[...]).astype(BF16)
    acc = None
    for j in range(D_FF // FF_CHUNK):
        lo = j * FF_CHUNK
        gate = _dot(h2, win_ref[:, lo:lo + FF_CHUNK])
        up = _dot(h2, win_ref[:, D_FF + lo:D_FF + lo + FF_CHUNK])
        t = _dot((_silu(gate) * up).astype(BF16), wout_ref[lo:lo + FF_CHUNK, :])
        acc = t if acc is None else acc + t
    o_ref[...] = x1 + g2_ref[...] * acc


def _na_mix_ffn_kernel(a_ref, *rest):
    _mix_ffn_tail(a_ref[...], *rest)


def _hg_mix_ffn_kernel(of_ref, ob_ref, sg_ref, gn_ref, *rest):
    o = of_ref[...].astype(F32) + ob_ref[...].astype(F32)
    parts = []
    for h in range(HG_HEADS):
        oh = o[:, h * HG_DK:(h + 1) * HG_DK]
        ms = jnp.mean(oh * oh, axis=-1, keepdims=True)
        parts.append(oh * lax.rsqrt(ms + EPS))
    a = jnp.concatenate(parts, axis=1) * gn_ref[...] * sg_ref[...].astype(F32)
    _mix_ffn_tail(a.astype(BF16), *rest)


def _mix_ffn(kernel, name, lead, lead_specs, x, g1, n2, sc2, sh2, g2, wo, win, wout, tm):
    B, T, _ = x.shape
    return pl.pallas_call(
        kernel,
        grid=(B, T // tm),
        in_specs=list(lead_specs) + [
            _tok_spec(tm), _mod_spec(g1), _resident((1, D_MODEL)), _mod_spec(sc2), _mod_spec(sh2), _mod_spec(g2),
            _resident(wo.shape), _resident(win.shape), _resident(wout.shape)],
        out_specs=_tok_spec(tm),
        out_shape=jax.ShapeDtypeStruct((B, T, D_MODEL), F32),
        compiler_params=_params("parallel", "parallel"),
        name=name,
    )(*lead, x, g1, n2, sc2, sh2, g2, wo, win, wout)


def _hg_proj_kernel(x_ref, g_ref, sc_ref, sh_ref, w_ref, lb_ref, q_ref, v_ref, sg_ref, kk_ref, lf_ref):
    h = _norm_mod(x_ref[...], g_ref[...], sc_ref[...], sh_ref[...]).astype(BF16)
    col = lambda n: w_ref[:, n * D_MODEL:(n + 1) * D_MODEL]
    q_ref[...] = _silu(_dot(h, col(0))).astype(BF16)
    v_ref[...] = _dot(h, col(1)).astype(BF16)
    sg_ref[...] = _silu(_dot(h, col(2))).astype(BF16)
    for d in range(2):
        f_pre = _dot(h, col(3 + d))
        lb = lb_ref[d]
        e = jnp.exp(-jnp.abs(f_pre))
        log_sig = jnp.minimum(f_pre, 0.0) - jnp.log1p(e)
        u = jnp.log(lb)
        w = jnp.log1p(-lb) + log_sig
        lf_ref[d] = jnp.maximum(u, w) + jnp.log1p(jnp.exp(-jnp.abs(u - w)))
        sig_neg = jnp.where(f_pre >= 0.0, e, 1.0) / (1.0 + e)
        kk_ref[d] = ((1.0 - lb) * sig_neg).astype(BF16)


def _hg_proj(x, g, sc, sh, w, lb, tm):
    B, T, _ = x.shape
    tok = jax.ShapeDtypeStruct((B, T, D_MODEL), BF16)
    two = pl.BlockSpec((2, None, tm, D_MODEL), lambda b, i: (0, b, i, 0))
    return pl.pallas_call(
        _hg_proj_kernel,
        grid=(B, T // tm),
        in_specs=[_tok_spec(tm), _resident((1, D_MODEL)), _mod_spec(sc), _mod_spec(sh),
                  _resident(w.shape), _resident(lb.shape)],
        out_specs=[_tok_spec(tm)] * 3 + [two, two],
        out_shape=[tok] * 3 + [jax.ShapeDtypeStruct((2, B, T, D_MODEL), BF16),
                               jax.ShapeDtypeStruct((2, B, T, D_MODEL), F32)],
        compiler_params=_params("parallel", "parallel"),
        name="hg_proj",
    )(x, g, sc, sh, w, lb)


def _hg_scan_kernel(q_ref, v_ref, kk_ref, lf_ref, tri_ref, s0_ref, o_ref, st_ref):
    @pl.when(pl.program_id(2) == 0)
    def _():
        st_ref[...] = s0_ref[...]

    tri = tri_ref[...]
    lf = lf_ref[...]
    b = jnp.dot(tri, lf, precision=lax.Precision.HIGHEST, preferred_element_type=F32)
    tot = jnp.sum(lf, axis=0, keepdims=True)
    mid = 0.5 * tot
    q = q_ref[...].astype(F32)
    kk = kk_ref[...].astype(F32)
    q_in = (q * jnp.exp(jnp.minimum(b - mid, EXP_CLAMP))).astype(BF16)
    k_in = (kk * jnp.exp(jnp.minimum(mid - b, EXP_CLAMP))).astype(BF16)
    q_st = (q * jnp.exp(b)).astype(BF16)
    k_st = (kk * jnp.exp(tot - b)).astype(BF16)
    v = v_ref[...]
    decay = jnp.exp(tot)
    outs = []
    for h in range(HG_HEADS):
        sl = slice(h * HG_DK, (h + 1) * HG_DK)
        st = st_ref[h]
        a = _dot_nt(q_in[:, sl], k_in[:, sl]) * tri
        o = _dot(a.astype(BF16), v[:, sl]) + _dot_nt(q_st[:, sl], st.astype(BF16))
        outs.append(o)
        st_ref[h] = st * decay[:, sl] + _dot_tn(v[:, sl], k_st[:, sl])
    o_ref[...] = jnp.concatenate(outs, axis=1).astype(BF16)


def _hg_scan(q, v, kk, lf, tri, s0):
    B, T, _ = q.shape
    C = SCAN_CHUNK
    nc = T // C

    def chunk(d, i):
        return i + d * (nc - 1 - 2 * i)

    tok = pl.BlockSpec((None, C, D_MODEL), lambda b, d, i: (b, chunk(d, i), 0))
    dtok = pl.BlockSpec((None, None, C, D_MODEL), lambda b, d, i: (d, b, chunk(d, i), 0))
    state = pl.BlockSpec((None, None, HG_HEADS, HG_DK, HG_DK), lambda b, d, i: (d, b, 0, 0, 0))
    return pl.pallas_call(
        _hg_scan_kernel,
        grid=(B, 2, nc),
        in_specs=[tok, tok, dtok, dtok, pl.BlockSpec((None, C, C), lambda b, d, i: (d, 0, 0)), state],
        out_specs=[dtok, state],
        out_shape=[jax.ShapeDtypeStruct((2, B, T, D_MODEL), BF16),
                   jax.ShapeDtypeStruct((2, B, HG_HEADS, HG_DK, HG_DK), F32)],
        compiler_params=_params("parallel", "parallel", "arbitrary"),
        name="hg_scan",
    )(q, v, kk, lf, tri, s0)


def _tile(n, pref):
    return pref if n % pref == 0 else n


def kernel(x, c, ctx, c_ctx, ada_w, ada_b, norm1_g, norm2_g, na_w_qkv, na_w_o, na_q_gain, na_k_gain, na_rpb,
           hg_w_in, hg_lower, hg_norm_g, hg_w_o, ffn_w_in, ffn_w_out):
    B, T, _ = x.shape
    L = ctx.shape[1]
    rows = T // GRID_W
    assert B + 1 <= 8 and T % Q_BLK == 0 and rows >= WIN_H and T % SCAN_CHUNK == 0 and L % SCAN_CHUNK == 0
    tm = _tile(T, 512)
    tl = _tile(L, 256)
    row = lambda a: a.reshape(1, -1).astype(F32)

    cvec = jnp.zeros((8, D_MODEL), F32).at[:B].set(c).at[B].set(c_ctx)
    mods = _ada(cvec, ada_w, ada_b)

    def mod_rows(i):
        lat = [m.reshape(B, 1, D_MODEL) for m in jnp.split(mods[i, :B], N_MOD, axis=-1)]
        cx = [m.reshape(1, 1, D_MODEL) for m in jnp.split(mods[i, B], N_MOD, axis=-1)]
        return lat, cx

    (sh1, sc1, g1, sh2, sc2, g2), (csh1, csc1, cg1, csh2, csc2, cg2) = mod_rows(0)
    w_qkv = na_w_qkv[0].astype(BF16)
    gq = row(jnp.tile(na_q_gain[0], NA_HEADS) * NA_HEAD_DIM ** -0.5)
    gk = row(jnp.tile(na_k_gain[0], NA_HEADS))
    head_of = np.arange(D_MODEL) // NA_HEAD_DIM
    red = jnp.asarray(head_of[:, None] == np.arange(LANES)[None, :], BF16)
    expand = jnp.asarray(np.arange(LANES)[:, None] == head_of[None, :], BF16)
    n1 = row(norm1_g[0])
    q_l, k_l, v_l = _qkv(x, n1, sc1, sh1, w_qkv, gq, gk, red, expand, tm)
    q_c, k_c, v_c = _qkv(ctx, n1, csc1, csh1, w_qkv, gq, gk, red, expand, tl)
    a_l = _attn(q_l, k_l, v_l, k_c, v_c, _attn_bias(na_rpb[0].astype(F32), rows))
    a_c = _attn_ctx(q_c, k_c, v_c)
    ffn = (na_w_o[0].astype(BF16), ffn_w_in[0].astype(BF16), ffn_w_out[0].astype(BF16))
    n2 = row(norm2_g[0])
    x_lat = _mix_ffn(_na_mix_ffn_kernel, "na_mix_ffn", (a_l,), (_tok_spec(tm),),
                     x, g1, n2, sc2, sh2, g2, *ffn, tm)
    x_ctx = _mix_ffn(_na_mix_ffn_kernel, "na_mix_ffn_ctx", (a_c,), (_tok_spec(tl),),
                     ctx, cg1, n2, csc2, csh2, cg2, *ffn, tl)

    (sh1, sc1, g1, sh2, sc2, g2), (csh1, csc1, _, _, _, _) = mod_rows(1)
    lbs = jnp.cumsum(jax.nn.softmax(hg_lower.astype(F32), axis=0), axis=0)
    lb = (lbs - lbs[:1])[1].reshape(2, 1, D_MODEL)
    w_in = hg_w_in[0].astype(BF16)
    n1 = row(norm1_g[1])
    t_idx = np.arange(SCAN_CHUNK)
    tri = jnp.asarray(np.stack([t_idx[None, :] <= t_idx[:, None], t_idx[None, :] >= t_idx[:, None]]), F32)
    q_c, v_c, _, kk_c, lf_c = _hg_proj(x_ctx, n1, csc1, csh1, w_in, lb, tl)
    q_l, v_l, sg_l, kk_l, lf_l = _hg_proj(x_lat, n1, sc1, sh1, w_in, lb, tm)
    zero = jnp.zeros((2, B, HG_HEADS, HG_DK, HG_DK), F32)
    _, s_ctx = _hg_scan(q_c, v_c, kk_c, lf_c, tri, zero)
    o_l, _ = _hg_scan(q_l, v_l, kk_l, lf_l, tri, s_ctx)
    ffn = (hg_w_o[0].astype(BF16), ffn_w_in[1].astype(BF16), ffn_w_out[1].astype(BF16))
    dir_spec = lambda d: pl.BlockSpec((None, None, tm, D_MODEL), lambda b, i: (d, b, i, 0))
    gn = row(jnp.tile(hg_norm_g[0], HG_HEADS))
    return _mix_ffn(_hg_mix_ffn_kernel, "hg_mix_ffn", (o_l, o_l, sg_l, gn),
                    (dir_spec(0), dir_spec(1), _tok_spec(tm), _resident((1, D_MODEL))),
                    x_lat, g1, row(norm2_g[1]), sc2, sh2, g2, *ffn, tm)
```

```python
import functools

import jax
import jax.numpy as jnp
import numpy as np
from jax import lax
from jax.experimental import pallas as pl
from jax.experimental.pallas import tpu as pltpu

F32 = jnp.float32
BF16 = jnp.bfloat16

D_MODEL = 1024
EPS = 1e-6
GRID_W = 64
WIN_H = 8
WIN_W = 16
NA_HEADS = 16
NA_HEAD_DIM = D_MODEL // NA_HEADS
HG_HEADS = 8
HG_DK = 128
D_FF = 2816
N_MOD = 6

LANES = 128
Q_ROWS = 4
Q_BLK = Q_ROWS * GRID_W
KEY_SLOTS = 3
ATTN_GROUP = 4
Q_BLOCKS_PER_STEP = 2
LOG2E = 1.4426950408889634
MASKED = -1e30
SCAN_CHUNK = 64
SCAN_SUB = 4
EXP2_CLAMP = 115.0
VMEM_LIMIT = 56 * 1024 * 1024


def _dot(a, b):
    return jnp.dot(a, b, preferred_element_type=F32)


def _dot_nt(a, b):
    return lax.dot_general(a, b, (((1,), (1,)), ((), ())), preferred_element_type=F32)


def _dot_tn(a, b):
    return lax.dot_general(a, b, (((0,), (0,)), ((), ())), preferred_element_type=F32)


def _silu(x):
    return x * (1.0 / (1.0 + jnp.exp(-x)))


def _norm_mod(xf, g, sc, sh):
    ms = jnp.mean(xf * xf, axis=-1, keepdims=True)
    return (xf * lax.rsqrt(ms + EPS) * g) * (1.0 + sc) + sh


def _split_bf16(x):
    hi = x.astype(BF16)
    lo = (x - hi.astype(F32)).astype(BF16)
    return hi, lo


def _resident(shape):
    zeros = (0,) * len(shape)
    return pl.BlockSpec(shape, lambda *_: zeros, pipeline_mode=pl.Buffered(1))


def _mod_spec(arr):
    if arr.shape[0] == 1:
        return pl.BlockSpec((None, 1, D_MODEL), lambda b, i: (0, 0, 0))
    return pl.BlockSpec((None, 1, D_MODEL), lambda b, i: (b, 0, 0))


def _tok_spec(tm, width=D_MODEL):
    return pl.BlockSpec((None, tm, width), lambda b, i: (b, i, 0))


def _params(*sem):
    return pltpu.CompilerParams(dimension_semantics=sem, vmem_limit_bytes=VMEM_LIMIT)


def _ada_kernel(c_ref, w_ref, b_ref, o_ref):
    s = _silu(c_ref[...]).astype(BF16)
    o_ref[...] = _dot(s, w_ref[...].astype(BF16)) + b_ref[...]


def _ada(cvec, ada_w, ada_b):
    depth, _, n = ada_w.shape
    tn = 1536
    return pl.pallas_call(
        _ada_kernel,
        grid=(depth, n // tn),
        in_specs=[pl.BlockSpec((8, D_MODEL), lambda l, j: (0, 0)),
                  pl.BlockSpec((None, D_MODEL, tn), lambda l, j: (l, 0, j)),
                  pl.BlockSpec((None, 1, tn), lambda l, j: (l, 0, j))],
        out_specs=pl.BlockSpec((None, 8, tn), lambda l, j: (l, 0, j)),
        out_shape=jax.ShapeDtypeStruct((depth, 8, n), F32),
        compiler_params=_params("arbitrary", "arbitrary"),
        name="ada",
    )(cvec, ada_w, ada_b.reshape(depth, 1, n))


def _qkv_kernel(x_ref, g_ref, sc_ref, sh_ref, w_ref, wvt_ref, gq_ref, gk_ref, red_ref, exp_ref,
                q_ref, k_ref, vt_ref):
    h = _norm_mod(x_ref[...], g_ref[...], sc_ref[...], sh_ref[...]).astype(BF16)

    def head_norm(z, gain):
        hi, lo = _split_bf16(z * z)
        ss = _dot(hi, red_ref[...]) + _dot(lo, red_ref[...])
        r = lax.rsqrt(ss * (1.0 / NA_HEAD_DIM) + EPS)
        rhi, rlo = _split_bf16(r)
        rex = _dot(rhi, exp_ref[...]) + _dot(rlo, exp_ref[...])
        return z * rex * gain

    q = _dot(h, w_ref[:, 0:D_MODEL])
    q_ref[...] = head_norm(q, gq_ref[...]).astype(BF16)
    k = _dot(h, w_ref[:, D_MODEL:2 * D_MODEL])
    k_ref[...] = head_norm(k, gk_ref[...]).astype(BF16)
    vt_ref[...] = _dot_nt(wvt_ref[...], h).astype(BF16)


def _qkv(x, g, sc, sh, w_qk, w_vt, gq, gk, red, expand, tm):
    B, T, _ = x.shape
    out = jax.ShapeDtypeStruct((B, T, D_MODEL), BF16)
    return pl.pallas_call(
        _qkv_kernel,
        grid=(B, T // tm),
        in_specs=[_tok_spec(tm), _resident((1, D_MODEL)), _mod_spec(sc), _mod_spec(sh),
                  _resident(w_qk.shape), _resident(w_vt.shape), _resident((1, D_MODEL)), _resident((1, D_MODEL)),
                  _resident(red.shape), _resident(expand.shape)],
        out_specs=[_tok_spec(tm), _tok_spec(tm), pl.BlockSpec((None, D_MODEL, tm), lambda b, i: (b, 0, i))],
        out_shape=[out, out, jax.ShapeDtypeStruct((B, D_MODEL, T), BF16)],
        compiler_params=_params("parallel", "parallel"),
        name="qkv",
    )(x, g, sc, sh, w_qk, w_vt, gq, gk, red, expand)


def _attend(tasks):
    ones_rows = 16

    def stage_scores(task):
        q_h, keys, _, biases = task
        s = []
        for key, bias in zip(keys, biases):
            t = _dot_nt(key(), q_h)
            s.append(t if bias is None else t + bias())
        m = functools.reduce(jnp.maximum, [jnp.max(t, axis=0, keepdims=True) for t in s])
        return s, m

    def stage_probs(sm):
        s, m = sm
        return [jnp.exp2(t - m).astype(BF16) for t in s]

    def stage_values(task, p):
        acc = None
        for vt, t in zip(task[2], p):
            v1 = jnp.concatenate([vt(), jnp.ones((ones_rows, t.shape[0]), BF16)], axis=0)
            part = _dot(v1, t)
            acc = part if acc is None else acc + part
        return acc[:NA_HEAD_DIM] * (1.0 / acc[NA_HEAD_DIM:NA_HEAD_DIM + 1])

    n = len(tasks)
    scores, probs, outs = {}, {}, []
    for step in range(n + 2):
        if step < n:
            scores[step] = stage_scores(tasks[step])
        if 0 <= step - 1 < n:
            probs[step - 1] = stage_probs(scores.pop(step - 1))
        if 0 <= step - 2 < n:
            outs.append(stage_values(tasks[step - 2], probs.pop(step - 2)))
    return outs


def _head_queries(q_grp):
    lane_head = lax.broadcasted_iota(jnp.int32, (1, q_grp.shape[1]), 1) // NA_HEAD_DIM
    return [jnp.where(lane_head == h, q_grp, jnp.zeros_like(q_grp)) for h in range(ATTN_GROUP)]


def _head_rows(ref, h):
    return lambda: ref[h * NA_HEAD_DIM:(h + 1) * NA_HEAD_DIM, :]


def _attn_kernel(q_ref, k0_ref, k1_ref, k2_ref, k3_ref, v0_ref, v1_ref, v2_ref, v3_ref, kc_ref, vc_ref,
                 bias0_ref, bias1_ref, o_ref):
    k_refs = (k0_ref, k1_ref, k2_ref, k3_ref)
    v_refs = (v0_ref, v1_ref, v2_ref, v3_ref)
    bias_refs = (bias0_ref, bias1_ref)
    tasks = []
    for jb in range(Q_BLOCKS_PER_STEP):
        q_heads = _head_queries(q_ref[jb * Q_BLK:(jb + 1) * Q_BLK, :])
        for h in range(ATTN_GROUP):
            keys = [(lambda r=r: r[...]) for r in k_refs[jb:jb + KEY_SLOTS]] + [lambda: kc_ref[...]]
            vts = [_head_rows(r, h) for r in v_refs[jb:jb + KEY_SLOTS]] + [_head_rows(vc_ref, h)]
            biases = [(lambda t=t, jb=jb, h=h: bias_refs[jb][h, t * Q_BLK:(t + 1) * Q_BLK, :])
                      for t in range(KEY_SLOTS)] + [None]
            tasks.append((q_heads[h], keys, vts, biases))
    outs = _attend(tasks)
    for jb in range(Q_BLOCKS_PER_STEP):
        o_t = jnp.concatenate(outs[jb * ATTN_GROUP:(jb + 1) * ATTN_GROUP], axis=0)
        o_ref[jb * Q_BLK:(jb + 1) * Q_BLK, :] = o_t.T.astype(BF16)


def _attn(q, k, vt, kc, vtc, bias):
    B, T, _ = q.shape
    L = kc.shape[1]
    nb = T // Q_BLK
    gw = ATTN_GROUP * NA_HEAD_DIM
    per = Q_BLOCKS_PER_STEP
    n_halo = per + KEY_SLOTS - 1

    def blk(i, d):
        return jnp.clip(per * i + d, 0, nb - 1)

    def variant(qb):
        return jnp.where(qb == 0, 0, jnp.where(qb == nb - 1, 2, 1))

    halo = [pl.BlockSpec((None, Q_BLK, gw), lambda g, b, i, d=d: (b, blk(i, d - 1), g)) for d in range(n_halo)]
    halo_t = [pl.BlockSpec((None, gw, Q_BLK), lambda g, b, i, d=d: (b, g, blk(i, d - 1))) for d in range(n_halo)]
    biases = [pl.BlockSpec((None, ATTN_GROUP, KEY_SLOTS * Q_BLK, Q_BLK),
                           lambda g, b, i, jb=jb: (variant(per * i + jb), g, 0, 0), pipeline_mode=pl.Buffered(1))
              for jb in range(per)]
    tok = pl.BlockSpec((None, per * Q_BLK, gw), lambda g, b, i: (b, i, g))
    return pl.pallas_call(
        _attn_kernel,
        grid=(D_MODEL // gw, B, nb // per),
        in_specs=[tok] + halo + halo_t + [pl.BlockSpec((None, L, gw), lambda g, b, i: (b, 0, g)),
                                         pl.BlockSpec((None, gw, L), lambda g, b, i: (b, g, 0))] + biases,
        out_specs=tok,
        out_shape=jax.ShapeDtypeStruct((B, T, D_MODEL), BF16),
        compiler_params=_params("parallel", "parallel", "parallel"),
        name="attn",
    )(q, *([k] * n_halo), *([vt] * n_halo), kc, vtc, *([bias] * per))


def _attn_ctx_kernel(q_ref, k_ref, vt_ref, o_ref):
    q_heads = _head_queries(q_ref[...])
    tasks = [(q_heads[h], [lambda: k_ref[...]], [_head_rows(vt_ref, h)], [None]) for h in range(ATTN_GROUP)]
    o_ref[...] = jnp.concatenate(_attend(tasks), axis=0).T.astype(BF16)


def _attn_ctx(q, k, vt):
    B, L, _ = q.shape
    gw = ATTN_GROUP * NA_HEAD_DIM
    spec = pl.BlockSpec((None, L, gw), lambda b, g: (b, 0, g))
    return pl.pallas_call(
        _attn_ctx_kernel,
        grid=(B, D_MODEL // gw),
        in_specs=[spec, spec, pl.BlockSpec((None, gw, L), lambda b, g: (b, g, 0))],
        out_specs=spec,
        out_shape=jax.ShapeDtypeStruct((B, L, D_MODEL), BF16),
        compiler_params=_params("parallel", "parallel"),
        name="attn_ctx",
    )(q, k, vt)


def _attn_bias(rpb, rows):
    n_heads = rpb.shape[0]
    n_slot = KEY_SLOTS * Q_ROWS
    cc = np.arange(GRID_W)[:, None]
    c = np.arange(GRID_W)[None, :]
    onehot = (cc - c + WIN_W - 1)[None] == np.arange(2 * WIN_W - 1)[:, None, None]
    toep = jnp.einsum('hdk,kxc->hdxc', rpb, jnp.asarray(onehot, F32), precision=lax.Precision.HIGHEST)
    lo_d = WIN_H - 1 - Q_ROWS
    vals = jnp.stack([toep[:, lo_d - a:lo_d - a + n_slot] for a in range(Q_ROWS)], axis=3)
    col_start = np.clip(c - WIN_W // 2, 0, GRID_W - WIN_W)
    col_ok = ((cc >= col_start) & (cc < col_start + WIN_W))[None, :, None, :]
    j = np.arange(n_slot)[:, None, None, None]
    a = np.arange(Q_ROWS)[None, None, :, None]
    kh = min(WIN_H, rows)
    lows = (np.full_like(a, Q_ROWS), a + Q_ROWS - kh // 2, np.full_like(a, 2 * Q_ROWS - kh))
    out = []
    for lo in lows:
        ok = np.broadcast_to((j >= lo) & (j < lo + kh) & col_ok, (n_slot, GRID_W, Q_ROWS, GRID_W))
        out.append(jnp.where(ok[None], vals, MASKED).reshape(n_heads, KEY_SLOTS * Q_BLK, Q_BLK))
    return jnp.stack(out)


FF_CHUNK = D_FF // 2


def _mix_ffn_tail(a, x_ref, g1_ref, n2_ref, sc2_ref, sh2_ref, g2_ref, wo_ref, win_ref, wout_ref, o_ref):
    x1 = x_ref[...] + g1_ref[...] * _dot(a, wo_ref[...])
    h2 = _norm_mod(x1, n2_ref[...], sc2_ref[...], sh2_ref[...]).astype(BF16)
    acc = None
    for j in range(D_FF // FF_CHUNK):
        lo = j * FF_CHUNK
        gate = _dot(h2, win_ref[:, lo:lo + FF_CHUNK])
        up = _dot(h2, win_ref[:, D_FF + lo:D_FF + lo + FF_CHUNK])
        t = _dot((_silu(gate) * up).astype(BF16), wout_ref[lo:lo + FF_CHUNK, :])
        acc = t if acc is None else acc + t
    o_ref[...] = x1 + g2_ref[...] * acc


def _na_mix_ffn_kernel(a_ref, *rest):
    _mix_ffn_tail(a_ref[...], *rest)


def _hg_mix_ffn_kernel(of_ref, ob_ref, sg_ref, gn_ref, *rest):
    o = of_ref[...].astype(F32) + ob_ref[...].astype(F32)
    parts = []
    for h in range(HG_HEADS):
        oh = o[:, h * HG_DK:(h + 1) * HG_DK]
        ms = jnp.mean(oh * oh, axis=-1, keepdims=True)
        parts.append(oh * lax.rsqrt(ms + EPS))
    a = jnp.concatenate(parts, axis=1) * gn_ref[...] * sg_ref[...].astype(F32)
    _mix_ffn_tail(a.astype(BF16), *rest)


def _mix_ffn(kernel, name, lead, lead_specs, x, g1, n2, sc2, sh2, g2, wo, win, wout, tm):
    B, T, _ = x.shape
    return pl.pallas_call(
        kernel,
        grid=(B, T // tm),
        in_specs=list(lead_specs) + [
            _tok_spec(tm), _mod_spec(g1), _resident((1, D_MODEL)), _mod_spec(sc2), _mod_spec(sh2), _mod_spec(g2),
            _resident(wo.shape), _resident(win.shape), _resident(wout.shape)],
        out_specs=_tok_spec(tm),
        out_shape=jax.ShapeDtypeStruct((B, T, D_MODEL), F32),
        compiler_params=_params("parallel", "parallel"),
        name=name,
    )(*lead, x, g1, n2, sc2, sh2, g2, wo, win, wout)


def _hg_proj_kernel(x_ref, g_ref, sc_ref, sh_ref, w_ref, lb_ref, q_ref, v_ref, sg_ref, kk_ref, lf_ref):
    h = _norm_mod(x_ref[...], g_ref[...], sc_ref[...], sh_ref[...]).astype(BF16)
    col = lambda n: w_ref[:, n * D_MODEL:(n + 1) * D_MODEL]
    q_ref[...] = _silu(_dot(h, col(0))).astype(BF16)
    v_ref[...] = _dot(h, col(1)).astype(BF16)
    sg_ref[...] = _silu(_dot(h, col(2))).astype(BF16)
    for d in range(2):
        f_pre = _dot(h, col(3 + d))
        lb = lb_ref[d]
        e = jnp.exp(-jnp.abs(f_pre))
        log_sig = jnp.minimum(f_pre, 0.0) - jnp.log1p(e)
        u = jnp.log(lb)
        w = jnp.log1p(-lb) + log_sig
        lf_ref[d] = (jnp.maximum(u, w) + jnp.log1p(jnp.exp(-jnp.abs(u - w)))) * LOG2E
        sig_neg = jnp.where(f_pre >= 0.0, e, 1.0) / (1.0 + e)
        kk_ref[d] = ((1.0 - lb) * sig_neg).astype(BF16)


def _hg_proj(x, g, sc, sh, w, lb, tm):
    B, T, _ = x.shape
    tok = jax.ShapeDtypeStruct((B, T, D_MODEL), BF16)
    two = pl.BlockSpec((2, None, tm, D_MODEL), lambda b, i: (0, b, i, 0))
    return pl.pallas_call(
        _hg_proj_kernel,
        grid=(B, T // tm),
        in_specs=[_tok_spec(tm), _resident((1, D_MODEL)), _mod_spec(sc), _mod_spec(sh),
                  _resident(w.shape), _resident(lb.shape)],
        out_specs=[_tok_spec(tm)] * 3 + [two, two],
        out_shape=[tok] * 3 + [jax.ShapeDtypeStruct((2, B, T, D_MODEL), BF16),
                               jax.ShapeDtypeStruct((2, B, T, D_MODEL), F32)],
        compiler_params=_params("parallel", "parallel"),
        name="hg_proj",
    )(x, g, sc, sh, w, lb)


def _hg_scan_kernel(qf_ref, vf_ref, kkf_ref, lff_ref, qb_ref, vb_ref, kkb_ref, lfb_ref, tri_ref, tri3_ref, s0_ref,
                    of_ref, ob_ref, st_ref):
    @pl.when(pl.program_id(1) == 0)
    def _():
        st_ref[...] = s0_ref[...]

    C = SCAN_CHUNK
    n_sub = qf_ref.shape[0] // C
    dirs = ((qf_ref, vf_ref, kkf_ref, lff_ref, of_ref), (qb_ref, vb_ref, kkb_ref, lfb_ref, ob_ref))
    streams = [(d, slice((j if d == 0 else n_sub - 1 - j) * C, (j if d == 0 else n_sub - 1 - j) * C + C))
               for j in range(n_sub) for d in range(2)]
    heads = [slice(h * HG_DK, (h + 1) * HG_DK) for h in range(HG_HEADS)]

    prep = []
    for d, rows in streams:
        q_ref, _, kk_ref, lf_ref, _ = dirs[d]
        lf = lf_ref[rows, :]
        hi = lf.astype(BF16)
        mid_part, lo = _split_bf16(lf - hi.astype(F32))
        b = _dot(tri3_ref[d], jnp.concatenate([hi, mid_part, lo], axis=0))
        tot = jnp.sum(lf, axis=0, keepdims=True)
        mid = 0.5 * tot
        q = q_ref[rows, :].astype(F32)
        kk = kk_ref[rows, :].astype(F32)
        prep.append(dict(
            q_in=(q * jnp.exp2(jnp.minimum(b - mid, EXP2_CLAMP))).astype(BF16),
            k_in=(kk * jnp.exp2(jnp.minimum(mid - b, EXP2_CLAMP))).astype(BF16),
            q_st=(q * jnp.exp2(b)).astype(BF16),
            k_st=(kk * jnp.exp2(tot - b)).astype(BF16),
            decay=jnp.exp2(tot)))

    intra = [[(_dot_nt(p["q_in"][:, sl], p["k_in"][:, sl]) * tri_ref[d]).astype(BF16) for sl in heads]
             for (d, _), p in zip(streams, prep)]
    o_intra = [[_dot(a, dirs[d][1][rows, sl]) for a, sl in zip(a_s, heads)]
               for (d, rows), a_s in zip(streams, intra)]
    update = [[_dot_tn(dirs[d][1][rows, sl], p["k_st"][:, sl]) for sl in heads]
              for (d, rows), p in zip(streams, prep)]

    for (d, rows), p, o_s, u_s in zip(streams, prep, o_intra, update):
        outs = []
        for h, sl in enumerate(heads):
            st = st_ref[d, h]
            outs.append(o_s[h] + _dot_nt(p["q_st"][:, sl], st.astype(BF16)))
            st_ref[d, h] = st * p["decay"][:, sl] + u_s[h]
        dirs[d][4][rows, :] = jnp.concatenate(outs, axis=1).astype(BF16)


def _hg_scan(q, v, kk, lf, tri, tri3, s0):
    B, T, _ = q.shape
    C = SCAN_CHUNK * (SCAN_SUB if T % (SCAN_CHUNK * SCAN_SUB) == 0 else 1)
    nc = T // C
    fwd = pl.BlockSpec((None, C, D_MODEL), lambda b, i: (b, i, 0))
    bwd = pl.BlockSpec((None, C, D_MODEL), lambda b, i: (b, nc - 1 - i, 0))
    dfwd = pl.BlockSpec((None, None, C, D_MODEL), lambda b, i: (0, b, i, 0))
    dbwd = pl.BlockSpec((None, None, C, D_MODEL), lambda b, i: (1, b, nc - 1 - i, 0))
    state = pl.BlockSpec((2, None, HG_HEADS, HG_DK, HG_DK), lambda b, i: (0, b, 0, 0, 0))
    tok = jax.ShapeDtypeStruct((B, T, D_MODEL), BF16)
    return pl.pallas_call(
        _hg_scan_kernel,
        grid=(B, nc),
        in_specs=[fwd, fwd, dfwd, dfwd, bwd, bwd, dbwd, dbwd, _resident(tri.shape), _resident(tri3.shape), state],
        out_specs=[fwd, bwd, state],
        out_shape=[tok, tok, jax.ShapeDtypeStruct((2, B, HG_HEADS, HG_DK, HG_DK), F32)],
        compiler_params=_params("parallel", "arbitrary"),
        name="hg_scan",
    )(q, v, kk, lf, q, v, kk, lf, tri, tri3, s0)


def _tile(n, pref):
    return pref if n % pref == 0 else n


def kernel(x, c, ctx, c_ctx, ada_w, ada_b, norm1_g, norm2_g, na_w_qkv, na_w_o, na_q_gain, na_k_gain, na_rpb,
           hg_w_in, hg_lower, hg_norm_g, hg_w_o, ffn_w_in, ffn_w_out):
    B, T, _ = x.shape
    L = ctx.shape[1]
    rows = T // GRID_W
    assert B + 1 <= 8 and T % (Q_BLOCKS_PER_STEP * Q_BLK) == 0 and rows >= WIN_H
    assert T % SCAN_CHUNK == 0 and L % SCAN_CHUNK == 0
    tm = _tile(T, 512)
    tl = _tile(L, 256)
    row = lambda a: a.reshape(1, -1).astype(F32)

    cvec = jnp.zeros((8, D_MODEL), F32).at[:B].set(c).at[B].set(c_ctx)
    mods = _ada(cvec, ada_w, ada_b)

    def mod_rows(i):
        lat = [m.reshape(B, 1, D_MODEL) for m in jnp.split(mods[i, :B], N_MOD, axis=-1)]
        cx = [m.reshape(1, 1, D_MODEL) for m in jnp.split(mods[i, B], N_MOD, axis=-1)]
        return lat, cx

    (sh1, sc1, g1, sh2, sc2, g2), (csh1, csc1, cg1, csh2, csc2, cg2) = mod_rows(0)
    w_qk = na_w_qkv[0, :, :2 * D_MODEL].astype(BF16)
    w_vt = na_w_qkv[0, :, 2 * D_MODEL:].T.astype(BF16)
    gq = row(jnp.tile(na_q_gain[0], NA_HEADS) * (NA_HEAD_DIM ** -0.5 * LOG2E))
    gk = row(jnp.tile(na_k_gain[0], NA_HEADS))
    head_of = np.arange(D_MODEL) // NA_HEAD_DIM
    red = jnp.asarray(head_of[:, None] == np.arange(LANES)[None, :], BF16)
    expand = jnp.asarray(np.arange(LANES)[:, None] == head_of[None, :], BF16)
    n1 = row(norm1_g[0])
    q_l, k_l, vt_l = _qkv(x, n1, sc1, sh1, w_qk, w_vt, gq, gk, red, expand, tm)
    q_c, k_c, vt_c = _qkv(ctx, n1, csc1, csh1, w_qk, w_vt, gq, gk, red, expand, tl)
    a_l = _attn(q_l, k_l, vt_l, k_c, vt_c, _attn_bias(na_rpb[0].astype(F32) * LOG2E, rows))
    a_c = _attn_ctx(q_c, k_c, vt_c)
    ffn = (na_w_o[0].astype(BF16), ffn_w_in[0].astype(BF16), ffn_w_out[0].astype(BF16))
    n2 = row(norm2_g[0])
    x_lat = _mix_ffn(_na_mix_ffn_kernel, "na_mix_ffn", (a_l,), (_tok_spec(tm),),
                     x, g1, n2, sc2, sh2, g2, *ffn, tm)
    x_ctx = _mix_ffn(_na_mix_ffn_kernel, "na_mix_ffn_ctx", (a_c,), (_tok_spec(tl),),
                     ctx, cg1, n2, csc2, csh2, cg2, *ffn, tl)

    (sh1, sc1, g1, sh2, sc2, g2), (csh1, csc1, _, _, _, _) = mod_rows(1)
    lbs = jnp.cumsum(jax.nn.softmax(hg_lower.astype(F32), axis=0), axis=0)
    lb = (lbs - lbs[:1])[1].reshape(2, 1, D_MODEL)
    w_in = hg_w_in[0].astype(BF16)
    n1 = row(norm1_g[1])
    t_idx = np.arange(SCAN_CHUNK)
    tri_np = np.stack([t_idx[None, :] <= t_idx[:, None], t_idx[None, :] >= t_idx[:, None]])
    tri = jnp.asarray(tri_np, F32)
    tri3 = jnp.asarray(np.tile(tri_np, (1, 1, 3)), BF16)
    q_c, v_c, _, kk_c, lf_c = _hg_proj(x_ctx, n1, csc1, csh1, w_in, lb, tl)
    q_l, v_l, sg_l, kk_l, lf_l = _hg_proj(x_lat, n1, sc1, sh1, w_in, lb, tm)
    zero = jnp.zeros((2, B, HG_HEADS, HG_DK, HG_DK), F32)
    _, _, s_ctx = _hg_scan(q_c, v_c, kk_c, lf_c, tri, tri3, zero)
    o_f, o_b, _ = _hg_scan(q_l, v_l, kk_l, lf_l, tri, tri3, s_ctx)
    ffn = (hg_w_o[0].astype(BF16), ffn_w_in[1].astype(BF16), ffn_w_out[1].astype(BF16))
    gn = row(jnp.tile(hg_norm_g[0], HG_HEADS))
    return _mix_ffn(_hg_mix_ffn_kernel, "hg_mix_ffn", (o_f, o_b, sg_l, gn),
                    (_tok_spec(tm), _tok_spec(tm), _tok_spec(tm), _resident((1, D_MODEL))),
                    x_lat, g1, row(norm2_g[1]), sc2, sh2, g2, *ffn, tm)
```

```python
import functools

import jax
import jax.numpy as jnp
import numpy as np
from jax import lax
from jax.experimental import pallas as pl
from jax.experimental.pallas import tpu as pltpu

F32 = jnp.float32
BF16 = jnp.bfloat16

D_MODEL = 1024
EPS = 1e-6
GRID_W = 64
WIN_H = 8
WIN_W = 16
NA_HEADS = 16
NA_HEAD_DIM = D_MODEL // NA_HEADS
HG_HEADS = 8
HG_DK = 128
D_FF = 2816
N_MOD = 6

LANES = 128
Q_ROWS = 4
Q_BLK = Q_ROWS * GRID_W
KEY_SLOTS = 3
ATTN_GROUP = 4
GROUP_W = ATTN_GROUP * NA_HEAD_DIM
BOUNDED_SPREAD = 100.0
Q_BLOCKS_PER_STEP = 2
LOG2E = 1.4426950408889634
MASKED = -1e30
SCAN_CHUNK = 64
SCAN_SUB = 4
EXP2_CLAMP = 115.0
GATE_FLOOR = 2.0 ** -100
VMEM_LIMIT = 56 * 1024 * 1024


def _dot(a, b):
    return jnp.dot(a, b, preferred_element_type=F32)


def _dot_nt(a, b):
    return lax.dot_general(a, b, (((1,), (1,)), ((), ())), preferred_element_type=F32)


def _dot_tn(a, b):
    return lax.dot_general(a, b, (((0,), (0,)), ((), ())), preferred_element_type=F32)


def _silu(x):
    half = 0.5 * x
    return half + half * jnp.tanh(half)


def _norm_mod(xf, g, sc, sh):
    ms = jnp.mean(xf * xf, axis=-1, keepdims=True)
    return (xf * lax.rsqrt(ms + EPS) * g) * (1.0 + sc) + sh


def _split_bf16(x):
    hi = x.astype(BF16)
    lo = (x - hi.astype(F32)).astype(BF16)
    return hi, lo


def _resident(shape):
    zeros = (0,) * len(shape)
    return pl.BlockSpec(shape, lambda *_: zeros, pipeline_mode=pl.Buffered(1))


def _mod_spec(arr):
    if arr.shape[0] == 1:
        return pl.BlockSpec((None, 1, D_MODEL), lambda b, i: (0, 0, 0))
    return pl.BlockSpec((None, 1, D_MODEL), lambda b, i: (b, 0, 0))


def _tok_spec(tm, width=D_MODEL):
    return pl.BlockSpec((None, tm, width), lambda b, i: (b, i, 0))


def _params(*sem):
    return pltpu.CompilerParams(dimension_semantics=sem, vmem_limit_bytes=VMEM_LIMIT)


def _ada_kernel(c_ref, w_ref, b_ref, o_ref):
    s = _silu(c_ref[...]).astype(BF16)
    o_ref[...] = _dot(s, w_ref[...].astype(BF16)) + b_ref[...]


def _ada(cvec, ada_w, ada_b):
    depth, _, n = ada_w.shape
    tn = 1536
    return pl.pallas_call(
        _ada_kernel,
        grid=(depth, n // tn),
        in_specs=[pl.BlockSpec((8, D_MODEL), lambda l, j: (0, 0)),
                  pl.BlockSpec((None, D_MODEL, tn), lambda l, j: (l, 0, j)),
                  pl.BlockSpec((None, 1, tn), lambda l, j: (l, 0, j))],
        out_specs=pl.BlockSpec((None, 8, tn), lambda l, j: (l, 0, j)),
        out_shape=jax.ShapeDtypeStruct((depth, 8, n), F32),
        compiler_params=_params("arbitrary", "arbitrary"),
        name="ada",
    )(cvec, ada_w, ada_b.reshape(depth, 1, n))


def _qkv_kernel(x_ref, g_ref, sc_ref, sh_ref, w_ref, wvt_ref, gq_ref, gk_ref, red_ref, exp_ref,
                q_ref, k_ref, vt_ref):
    h = _norm_mod(x_ref[...], g_ref[...], sc_ref[...], sh_ref[...]).astype(BF16)

    def head_norm(z, gain, out_ref):
        ss = _dot((z * z).astype(BF16), red_ref[...])
        r = lax.rsqrt(ss * (1.0 / NA_HEAD_DIM) + EPS)
        rex = _dot(jnp.concatenate(_split_bf16(r), axis=1), exp_ref[...])
        zn = (z * rex * gain).astype(BF16)
        for g in range(D_MODEL // GROUP_W):
            out_ref[g] = zn[:, g * GROUP_W:(g + 1) * GROUP_W]

    q = _dot(h, w_ref[:, 0:D_MODEL])
    k = _dot(h, w_ref[:, D_MODEL:2 * D_MODEL])
    head_norm(q, gq_ref[...], q_ref)
    vt = _dot_nt(wvt_ref[...], h).astype(BF16)
    head_norm(k, gk_ref[...], k_ref)
    for j in range(vt_ref.shape[0]):
        vt_ref[j] = vt[:, j * Q_BLK:(j + 1) * Q_BLK]


def _qkv(x, g, sc, sh, w_qk, w_vt, gq, gk, red, expand, tm):
    B, T, _ = x.shape
    n_grp = D_MODEL // GROUP_W
    grp = jax.ShapeDtypeStruct((n_grp, B, T, GROUP_W), BF16)
    grp_spec = pl.BlockSpec((n_grp, None, tm, GROUP_W), lambda b, i: (0, b, i, 0))
    return pl.pallas_call(
        _qkv_kernel,
        grid=(B, T // tm),
        in_specs=[_tok_spec(tm), _resident((1, D_MODEL)), _mod_spec(sc), _mod_spec(sh),
                  _resident(w_qk.shape), _resident(w_vt.shape), _resident((1, D_MODEL)), _resident((1, D_MODEL)),
                  _resident(red.shape), _resident(expand.shape)],
        out_specs=[grp_spec, grp_spec,
                   pl.BlockSpec((None, tm // Q_BLK, D_MODEL, Q_BLK), lambda b, i: (b, i, 0, 0))],
        out_shape=[grp, grp, jax.ShapeDtypeStruct((B, T // Q_BLK, D_MODEL, Q_BLK), BF16)],
        compiler_params=_params("parallel", "parallel"),
        name="qkv",
    )(x, g, sc, sh, w_qk, w_vt, gq, gk, red, expand)


def _attend(tasks):
    def stage_scores(task):
        q_h, keys, _, biases = task
        s = [_dot_nt(key(), q_h) + bias() for key, bias in zip(keys, biases)]
        m = functools.reduce(jnp.maximum, [jnp.max(t, axis=0, keepdims=True) for t in s])
        return s, m

    def stage_probs(sm):
        s, m = sm
        return [jnp.exp2(t - m).astype(BF16) for t in s]

    n = len(tasks)
    scores, probs, outs = {}, {}, []
    for step in range(n + 2):
        if step < n:
            scores[step] = stage_scores(tasks[step])
        if 0 <= step - 1 < n:
            probs[step - 1] = stage_probs(scores.pop(step - 1))
        if 0 <= step - 2 < n:
            outs.append(_weighted_values(tasks[step - 2], probs.pop(step - 2)))
    return outs


def _attend_bounded(tasks):
    def stage_probs(task):
        q_h, keys, _, biases = task
        return [jnp.exp2(_dot_nt(key(), q_h) + bias()).astype(BF16) for key, bias in zip(keys, biases)]

    n = len(tasks)
    probs, outs = {}, []
    for step in range(n + 1):
        if step < n:
            probs[step] = stage_probs(tasks[step])
        if step >= 1:
            outs.append(_weighted_values(tasks[step - 1], probs.pop(step - 1)))
    return outs


def _weighted_values(task, p):
    ones_rows = 16
    acc = None
    for vt, t in zip(task[2], p):
        v1 = jnp.concatenate([vt(), jnp.ones((ones_rows, t.shape[0]), BF16)], axis=0)
        part = _dot(v1, t)
        acc = part if acc is None else acc + part
    return acc[:NA_HEAD_DIM] * (1.0 / acc[NA_HEAD_DIM:NA_HEAD_DIM + 1])


def _head_queries(q_grp):
    lane_head = lax.broadcasted_iota(jnp.int32, (1, q_grp.shape[1]), 1) // NA_HEAD_DIM
    return [jnp.where(lane_head == h, q_grp, jnp.zeros_like(q_grp)) for h in range(ATTN_GROUP)]


def _head_rows(ref, h):
    return lambda: ref[h * NA_HEAD_DIM:(h + 1) * NA_HEAD_DIM, :]


def _attn_kernel(shift_ref, q_ref, k0_ref, k1_ref, k2_ref, k3_ref, v0_ref, v1_ref, v2_ref, v3_ref, kc_ref, vc_ref,
                 bias0_ref, bias1_ref, o_ref):
    k_refs = (k0_ref, k1_ref, k2_ref, k3_ref)
    v_refs = (v0_ref, v1_ref, v2_ref, v3_ref)
    bias_refs = (bias0_ref, bias1_ref)
    n_ctx = kc_ref.shape[0] // Q_BLK
    shift = shift_ref[0]
    tasks = []
    for jb in range(Q_BLOCKS_PER_STEP):
        q_heads = _head_queries(q_ref[jb * Q_BLK:(jb + 1) * Q_BLK, :])
        for h in range(ATTN_GROUP):
            keys = ([(lambda r=r: r[...]) for r in k_refs[jb:jb + KEY_SLOTS]]
                    + [(lambda t=t: kc_ref[t * Q_BLK:(t + 1) * Q_BLK, :]) for t in range(n_ctx)])
            vts = ([_head_rows(r, h) for r in v_refs[jb:jb + KEY_SLOTS]]
                   + [_head_rows(vc_ref.at[t], h) for t in range(n_ctx)])
            biases = ([(lambda t=t, jb=jb, h=h: bias_refs[jb][h, t * Q_BLK:(t + 1) * Q_BLK, :])
                       for t in range(KEY_SLOTS)] + [lambda: shift] * n_ctx)
            tasks.append((q_heads[h], keys, vts, biases))

    def finish(outs):
        for jb in range(Q_BLOCKS_PER_STEP):
            o_t = jnp.concatenate(outs[jb * ATTN_GROUP:(jb + 1) * ATTN_GROUP], axis=0)
            o_ref[jb * Q_BLK:(jb + 1) * Q_BLK, :] = o_t.T.astype(BF16)

    bounded = shift_ref[1] > 0.5

    @pl.when(bounded)
    def _():
        finish(_attend_bounded(tasks))

    @pl.when(jnp.logical_not(bounded))
    def _():
        finish(_attend(tasks))


def _attn(shift, q, k, vt, kc, vtc, bias):
    n_grp, B, T, gw = q.shape
    L = kc.shape[2]
    nb = T // Q_BLK
    per = Q_BLOCKS_PER_STEP
    n_halo = per + KEY_SLOTS - 1

    def blk(i, d):
        return jnp.clip(per * i + d, 0, nb - 1)

    def variant(qb):
        return jnp.where(qb == 0, 0, jnp.where(qb == nb - 1, 2, 1))

    halo = [pl.BlockSpec((None, None, Q_BLK, gw), lambda g, b, i, d=d: (g, b, blk(i, d - 1), 0))
            for d in range(n_halo)]
    halo_t = [pl.BlockSpec((None, None, gw, Q_BLK), lambda g, b, i, d=d: (b, blk(i, d - 1), g, 0))
              for d in range(n_halo)]
    biases = [pl.BlockSpec((None, ATTN_GROUP, KEY_SLOTS * Q_BLK, Q_BLK),
                           lambda g, b, i, jb=jb: (variant(per * i + jb), g, 0, 0), pipeline_mode=pl.Buffered(1))
              for jb in range(per)]
    tok = pl.BlockSpec((None, None, per * Q_BLK, gw), lambda g, b, i: (g, b, i, 0))
    return pl.pallas_call(
        _attn_kernel,
        grid=(n_grp, B, nb // per),
        in_specs=[pl.BlockSpec(memory_space=pltpu.SMEM), tok] + halo + halo_t
                 + [pl.BlockSpec((None, None, L, gw), lambda g, b, i: (g, b, 0, 0)),
                    pl.BlockSpec((None, L // Q_BLK, gw, Q_BLK), lambda g, b, i: (b, 0, g, 0))] + biases,
        out_specs=tok,
        out_shape=jax.ShapeDtypeStruct((n_grp, B, T, gw), BF16),
        compiler_params=_params("parallel", "parallel", "parallel"),
        name="attn",
    )(shift, q, *([k] * n_halo), *([vt] * n_halo), kc, vtc, *([bias] * per))


def _attn_ctx_kernel(q_ref, k_ref, vt_ref, o_ref):
    q_heads = _head_queries(q_ref[...])
    n_ctx = k_ref.shape[0] // Q_BLK
    tasks = [(q_heads[h],
              [(lambda t=t: k_ref[t * Q_BLK:(t + 1) * Q_BLK, :]) for t in range(n_ctx)],
              [_head_rows(vt_ref.at[t], h) for t in range(n_ctx)],
              [lambda: 0.0] * n_ctx) for h in range(ATTN_GROUP)]
    o_ref[...] = jnp.concatenate(_attend(tasks), axis=0).T.astype(BF16)


def _attn_ctx(q, k, vt):
    n_grp, B, L, gw = q.shape
    spec = pl.BlockSpec((None, None, L, gw), lambda b, g: (g, b, 0, 0))
    return pl.pallas_call(
        _attn_ctx_kernel,
        grid=(B, n_grp),
        in_specs=[spec, spec, pl.BlockSpec((None, L // Q_BLK, gw, Q_BLK), lambda b, g: (b, 0, g, 0))],
        out_specs=spec,
        out_shape=jax.ShapeDtypeStruct((n_grp, B, L, gw), BF16),
        compiler_params=_params("parallel", "parallel"),
        name="attn_ctx",
    )(q, k, vt)


def _attn_bias(rpb, rows, shift):
    n_heads = rpb.shape[0]
    n_slot = KEY_SLOTS * Q_ROWS
    cc = np.arange(GRID_W)[:, None]
    c = np.arange(GRID_W)[None, :]
    onehot = (cc - c + WIN_W - 1)[None] == np.arange(2 * WIN_W - 1)[:, None, None]
    toep = jnp.einsum('hdk,kxc->hdxc', rpb, jnp.asarray(onehot, F32), precision=lax.Precision.HIGHEST)
    lo_d = WIN_H - 1 - Q_ROWS
    vals = jnp.stack([toep[:, lo_d - a:lo_d - a + n_slot] for a in range(Q_ROWS)], axis=3)
    col_start = np.clip(c - WIN_W // 2, 0, GRID_W - WIN_W)
    col_ok = ((cc >= col_start) & (cc < col_start + WIN_W))[None, :, None, :]
    j = np.arange(n_slot)[:, None, None, None]
    a = np.arange(Q_ROWS)[None, None, :, None]
    kh = min(WIN_H, rows)
    lows = (np.full_like(a, Q_ROWS), a + Q_ROWS - kh // 2, np.full_like(a, 2 * Q_ROWS - kh))
    out = []
    for lo in lows:
        ok = np.broadcast_to((j >= lo) & (j < lo + kh) & col_ok, (n_slot, GRID_W, Q_ROWS, GRID_W))
        out.append(jnp.where(ok[None], vals + shift, MASKED).reshape(n_heads, KEY_SLOTS * Q_BLK, Q_BLK))
    return jnp.stack(out)


FF_CHUNK = D_FF // 2


def _mix_ffn_tail(mix, x_ref, g1_ref, n2_ref, sc2_ref, sh2_ref, g2_ref, wo_ref, win_ref, wout_ref, o_ref):
    x1 = x_ref[...] + g1_ref[...] * mix(wo_ref)
    h2 = _norm_mod(x1, n2_ref[...], sc2_ref[...], sh2_ref[...]).astype(BF16)
    acc = None
    for j in range(D_FF // FF_CHUNK):
        lo = j * FF_CHUNK
        gate = _dot(h2, win_ref[:, lo:lo + FF_CHUNK])
        up = _dot(h2, win_ref[:, D_FF + lo:D_FF + lo + FF_CHUNK])
        t = _dot((_silu(gate) * up).astype(BF16), wout_ref[lo:lo + FF_CHUNK, :])
        acc = t if acc is None else acc + t
    o_ref[...] = x1 + g2_ref[...] * acc


def _na_mix_ffn_kernel(a_ref, *rest):
    def mix(wo_ref):
        parts = [_dot(a_ref[g], wo_ref[g * GROUP_W:(g + 1) * GROUP_W, :]) for g in range(a_ref.shape[0])]
        return functools.reduce(jnp.add, parts)

    _mix_ffn_tail(mix, *rest)


def _hg_mix_ffn_kernel(of_ref, ob_ref, sg_ref, gn_ref, *rest):
    o = of_ref[...].astype(F32) + ob_ref[...].astype(F32)
    parts = []
    for h in range(HG_HEADS):
        oh = o[:, h * HG_DK:(h + 1) * HG_DK]
        ms = jnp.mean(oh * oh, axis=-1, keepdims=True)
        parts.append(oh * lax.rsqrt(ms + EPS))
    a = (jnp.concatenate(parts, axis=1) * gn_ref[...] * sg_ref[...].astype(F32)).astype(BF16)
    _mix_ffn_tail(lambda wo_ref: _dot(a, wo_ref[...]), *rest)


def _mix_ffn(kernel, name, lead, lead_specs, x, g1, n2, sc2, sh2, g2, wo, win, wout, tm):
    B, T, _ = x.shape
    return pl.pallas_call(
        kernel,
        grid=(B, T // tm),
        in_specs=list(lead_specs) + [
            _tok_spec(tm), _mod_spec(g1), _resident((1, D_MODEL)), _mod_spec(sc2), _mod_spec(sh2), _mod_spec(g2),
            _resident(wo.shape), _resident(win.shape), _resident(wout.shape)],
        out_specs=_tok_spec(tm),
        out_shape=jax.ShapeDtypeStruct((B, T, D_MODEL), F32),
        compiler_params=_params("parallel", "parallel"),
        name=name,
    )(*lead, x, g1, n2, sc2, sh2, g2, wo, win, wout)


def _hg_proj_kernel(x_ref, g_ref, sc_ref, sh_ref, w_ref, lb_ref, tri3_ref, q_ref, v_ref, sg_ref, kk_ref, cum_ref):
    h = _norm_mod(x_ref[...], g_ref[...], sc_ref[...], sh_ref[...]).astype(BF16)

    def project(n):
        return _dot(h, w_ref[:, n * D_MODEL:(n + 1) * D_MODEL])

    def gates(d, f_pre):
        lb = lb_ref[d]
        half = 0.5 * (1.0 - lb)
        th = jnp.tanh(0.5 * f_pre)
        f = jnp.maximum(0.5 * (1.0 + lb) + half * th, GATE_FLOOR)
        kk_ref[d] = (half - half * th).astype(BF16)
        lf = jnp.log2(f)
        hi = lf.astype(BF16)
        mid, lo = _split_bf16(lf - hi.astype(F32))
        for c in range(f_pre.shape[0] // SCAN_CHUNK):
            rows = slice(c * SCAN_CHUNK, (c + 1) * SCAN_CHUNK)
            parts = jnp.concatenate([hi[rows], mid[rows], lo[rows]], axis=0)
            cum_ref[d, rows, :] = _dot(tri3_ref[d], parts)

    tails = [lambda z: q_ref.__setitem__(Ellipsis, _silu(z).astype(BF16)),
             lambda z: v_ref.__setitem__(Ellipsis, z.astype(BF16)),
             lambda z: sg_ref.__setitem__(Ellipsis, _silu(z).astype(BF16)),
             functools.partial(gates, 0), functools.partial(gates, 1)]
    pending = project(0)
    for n, tail in enumerate(tails):
        z = pending
        if n + 1 < len(tails):
            pending = project(n + 1)
        tail(z)


def _hg_proj(x, g, sc, sh, w, lb, tri3, tm):
    B, T, _ = x.shape
    tok = jax.ShapeDtypeStruct((B, T, D_MODEL), BF16)
    two = pl.BlockSpec((2, None, tm, D_MODEL), lambda b, i: (0, b, i, 0))
    return pl.pallas_call(
        _hg_proj_kernel,
        grid=(B, T // tm),
        in_specs=[_tok_spec(tm), _resident((1, D_MODEL)), _mod_spec(sc), _mod_spec(sh),
                  _resident(w.shape), _resident(lb.shape), _resident(tri3.shape)],
        out_specs=[_tok_spec(tm)] * 3 + [two, two],
        out_shape=[tok] * 3 + [jax.ShapeDtypeStruct((2, B, T, D_MODEL), BF16),
                               jax.ShapeDtypeStruct((2, B, T, D_MODEL), F32)],
        compiler_params=_params("parallel", "parallel"),
        name="hg_proj",
    )(x, g, sc, sh, w, lb, tri3)


def _hg_scan_kernel(qf_ref, vf_ref, kkf_ref, cumf_ref, qb_ref, vb_ref, kkb_ref, cumb_ref, tri_ref, s0_ref,
                    of_ref, ob_ref, st_ref):
    @pl.when(pl.program_id(1) == 0)
    def _():
        st_ref[...] = s0_ref[...]

    C = SCAN_CHUNK
    n_sub = qf_ref.shape[0] // C
    dirs = ((qf_ref, vf_ref, kkf_ref, cumf_ref, of_ref), (qb_ref, vb_ref, kkb_ref, cumb_ref, ob_ref))
    streams = [(d, slice((j if d == 0 else n_sub - 1 - j) * C, (j if d == 0 else n_sub - 1 - j) * C + C))
               for j in range(n_sub) for d in range(2)]
    heads = [slice(h * HG_DK, (h + 1) * HG_DK) for h in range(HG_HEADS)]

    prep = []
    for d, rows in streams:
        q_ref, _, kk_ref, cum_ref, _ = dirs[d]
        b = cum_ref[rows, :]
        last = rows.stop - 1 if d == 0 else rows.start
        tot = cum_ref[last:last + 1, :]
        mid = 0.5 * tot
        q = q_ref[rows, :].astype(F32)
        kk = kk_ref[rows, :].astype(F32)
        prep.append(dict(
            q_in=(q * jnp.exp2(jnp.minimum(b - mid, EXP2_CLAMP))).astype(BF16),
            k_in=(kk * jnp.exp2(jnp.minimum(mid - b, EXP2_CLAMP))).astype(BF16),
            q_st=(q * jnp.exp2(b)).astype(BF16),
            k_st=(kk * jnp.exp2(tot - b)).astype(BF16),
            decay=jnp.exp2(tot)))

    intra = [[(_dot_nt(p["q_in"][:, sl], p["k_in"][:, sl]) * tri_ref[d]).astype(BF16) for sl in heads]
             for (d, _), p in zip(streams, prep)]
    o_intra = [[_dot(a, dirs[d][1][rows, sl]) for a, sl in zip(a_s, heads)]
               for (d, rows), a_s in zip(streams, intra)]
    update = [[_dot_tn(dirs[d][1][rows, sl], p["k_st"][:, sl]) for sl in heads]
              for (d, rows), p in zip(streams, prep)]

    for (d, rows), p, o_s, u_s in zip(streams, prep, o_intra, update):
        outs = []
        for h, sl in enumerate(heads):
            st = st_ref[d, h]
            outs.append(o_s[h] + _dot_nt(p["q_st"][:, sl], st.astype(BF16)))
            st_ref[d, h] = st * p["decay"][:, sl] + u_s[h]
        dirs[d][4][rows, :] = jnp.concatenate(outs, axis=1).astype(BF16)


def _hg_scan(q, v, kk, cum, tri, s0):
    B, T, _ = q.shape
    C = SCAN_CHUNK * (SCAN_SUB if T % (SCAN_CHUNK * SCAN_SUB) == 0 else 1)
    nc = T // C
    fwd = pl.BlockSpec((None, C, D_MODEL), lambda b, i: (b, i, 0))
    bwd = pl.BlockSpec((None, C, D_MODEL), lambda b, i: (b, nc - 1 - i, 0))
    dfwd = pl.BlockSpec((None, None, C, D_MODEL), lambda b, i: (0, b, i, 0))
    dbwd = pl.BlockSpec((None, None, C, D_MODEL), lambda b, i: (1, b, nc - 1 - i, 0))
    state = pl.BlockSpec((2, None, HG_HEADS, HG_DK, HG_DK), lambda b, i: (0, b, 0, 0, 0))
    tok = jax.ShapeDtypeStruct((B, T, D_MODEL), BF16)
    return pl.pallas_call(
        _hg_scan_kernel,
        grid=(B, nc),
        in_specs=[fwd, fwd, dfwd, dfwd, bwd, bwd, dbwd, dbwd, _resident(tri.shape), state],
        out_specs=[fwd, bwd, state],
        out_shape=[tok, tok, jax.ShapeDtypeStruct((2, B, HG_HEADS, HG_DK, HG_DK), F32)],
        compiler_params=_params("parallel", "arbitrary"),
        name="hg_scan",
    )(q, v, kk, cum, q, v, kk, cum, tri, s0)


def _tile(n, pref):
    return pref if n % pref == 0 else n


def kernel(x, c, ctx, c_ctx, ada_w, ada_b, norm1_g, norm2_g, na_w_qkv, na_w_o, na_q_gain, na_k_gain, na_rpb,
           hg_w_in, hg_lower, hg_norm_g, hg_w_o, ffn_w_in, ffn_w_out):
    B, T, _ = x.shape
    L = ctx.shape[1]
    rows = T // GRID_W
    assert B + 1 <= 8 and T % (Q_BLOCKS_PER_STEP * Q_BLK) == 0 and rows >= WIN_H
    assert T % SCAN_CHUNK == 0 and L % Q_BLK == 0
    tm = _tile(T, 512)
    tl = _tile(L, 256)
    row = lambda a: a.reshape(1, -1).astype(F32)

    cvec = jnp.zeros((8, D_MODEL), F32).at[:B].set(c).at[B].set(c_ctx)
    mods = _ada(cvec, ada_w, ada_b)

    def mod_rows(i):
        lat = [m.reshape(B, 1, D_MODEL) for m in jnp.split(mods[i, :B], N_MOD, axis=-1)]
        cx = [m.reshape(1, 1, D_MODEL) for m in jnp.split(mods[i, B], N_MOD, axis=-1)]
        return lat, cx

    (sh1, sc1, g1, sh2, sc2, g2), (csh1, csc1, cg1, csh2, csc2, cg2) = mod_rows(0)
    w_qk = na_w_qkv[0, :, :2 * D_MODEL].astype(BF16)
    w_vt = na_w_qkv[0, :, 2 * D_MODEL:].T.astype(BF16)
    gq = row(jnp.tile(na_q_gain[0], NA_HEADS) * (NA_HEAD_DIM ** -0.5 * LOG2E))
    gk = row(jnp.tile(na_k_gain[0], NA_HEADS))
    head_of = np.arange(D_MODEL) // NA_HEAD_DIM
    red = jnp.asarray(head_of[:, None] == np.arange(LANES)[None, :], BF16)
    expand = jnp.asarray(np.arange(2 * LANES)[:, None] % LANES == head_of[None, :], BF16)
    n1 = row(norm1_g[0])
    q_l, k_l, vt_l = _qkv(x, n1, sc1, sh1, w_qk, w_vt, gq, gk, red, expand, tm)
    q_c, k_c, vt_c = _qkv(ctx, n1, csc1, csh1, w_qk, w_vt, gq, gk, red, expand, tl)
    rpb2 = na_rpb[0].astype(F32) * LOG2E
    qk_max = NA_HEAD_DIM * jnp.max(jnp.abs(gq)) * jnp.max(jnp.abs(gk))
    bound = qk_max + jnp.maximum(jnp.max(rpb2), 0.0)
    shift = jnp.stack([-bound, (bound + qk_max < BOUNDED_SPREAD).astype(F32)])
    a_l = _attn(shift, q_l, k_l, vt_l, k_c, vt_c, _attn_bias(rpb2, rows, -bound))
    a_c = _attn_ctx(q_c, k_c, vt_c)
    ffn = (na_w_o[0].astype(BF16), ffn_w_in[0].astype(BF16), ffn_w_out[0].astype(BF16))
    n2 = row(norm2_g[0])
    grp_spec = lambda t: pl.BlockSpec((D_MODEL // GROUP_W, None, t, GROUP_W), lambda b, i: (0, b, i, 0))
    x_lat = _mix_ffn(_na_mix_ffn_kernel, "na_mix_ffn", (a_l,), (grp_spec(tm),),
                     x, g1, n2, sc2, sh2, g2, *ffn, tm)
    x_ctx = _mix_ffn(_na_mix_ffn_kernel, "na_mix_ffn_ctx", (a_c,), (grp_spec(tl),),
                     ctx, cg1, n2, csc2, csh2, cg2, *ffn, tl)

    (sh1, sc1, g1, sh2, sc2, g2), (csh1, csc1, _, _, _, _) = mod_rows(1)
    lbs = jnp.cumsum(jax.nn.softmax(hg_lower.astype(F32), axis=0), axis=0)
    lb = (lbs - lbs[:1])[1].reshape(2, 1, D_MODEL)
    w_in = hg_w_in[0].astype(BF16)
    n1 = row(norm1_g[1])
    t_idx = np.arange(SCAN_CHUNK)
    tri_np = np.stack([t_idx[None, :] <= t_idx[:, None], t_idx[None, :] >= t_idx[:, None]])
    tri = jnp.asarray(tri_np, F32)
    tri3 = jnp.asarray(np.tile(tri_np, (1, 1, 3)), BF16)
    q_c, v_c, _, kk_c, cum_c = _hg_proj(x_ctx, n1, csc1, csh1, w_in, lb, tri3, tl)
    q_l, v_l, sg_l, kk_l, cum_l = _hg_proj(x_lat, n1, sc1, sh1, w_in, lb, tri3, tm)
    zero = jnp.zeros((2, B, HG_HEADS, HG_DK, HG_DK), F32)
    _, _, s_ctx = _hg_scan(q_c, v_c, kk_c, cum_c, tri, zero)
    o_f, o_b, _ = _hg_scan(q_l, v_l, kk_l, cum_l, tri, s_ctx)
    ffn = (hg_w_o[0].astype(BF16), ffn_w_in[1].astype(BF16), ffn_w_out[1].astype(BF16))
    gn = row(jnp.tile(hg_norm_g[0], HG_HEADS))
    return _mix_ffn(_hg_mix_ffn_kernel, "hg_mix_ffn", (o_f, o_b, sg_l, gn),
                    (_tok_spec(tm), _tok_spec(tm), _tok_spec(tm), _resident((1, D_MODEL))),
                    x_lat, g1, row(norm2_g[1]), sc2, sh2, g2, *ffn, tm)
```

```python
import functools

import jax
import jax.numpy as jnp
import numpy as np
from jax import lax
from jax.experimental import pallas as pl
from jax.experimental.pallas import tpu as pltpu

F32 = jnp.float32
BF16 = jnp.bfloat16

D_MODEL = 1024
EPS = 1e-6
GRID_W = 64
WIN_H = 8
WIN_W = 16
NA_HEADS = 16
NA_HEAD_DIM = D_MODEL // NA_HEADS
HG_HEADS = 8
HG_DK = 128
D_FF = 2816
N_MOD = 6

LANES = 128
Q_ROWS = 4
Q_BLK = Q_ROWS * GRID_W
KEY_SLOTS = 3
ATTN_GROUP = 4
GROUP_W = ATTN_GROUP * NA_HEAD_DIM
BOUNDED_SPREAD = 100.0
Q_BLOCKS_PER_STEP = 2
LOG2E = 1.4426950408889634
MASKED = -1e30
SCAN_CHUNK = 64
SCAN_SUB = 4
FACTOR_RANGE = 100.0
GATE_FLOOR = 2.0 ** -100
VMEM_LIMIT = 56 * 1024 * 1024


def _dot(a, b):
    return jnp.dot(a, b, preferred_element_type=F32)


def _dot_nt(a, b):
    return lax.dot_general(a, b, (((1,), (1,)), ((), ())), preferred_element_type=F32)


def _dot_tn(a, b):
    return lax.dot_general(a, b, (((0,), (0,)), ((), ())), preferred_element_type=F32)


def _silu(x):
    half = 0.5 * x
    return half + half * jnp.tanh(half)


def _norm_mod(xf, g, sc, sh):
    ms = jnp.mean(xf * xf, axis=-1, keepdims=True)
    return (xf * lax.rsqrt(ms + EPS) * g) * (1.0 + sc) + sh


def _split_bf16(x):
    hi = x.astype(BF16)
    lo = (x - hi.astype(F32)).astype(BF16)
    return hi, lo


def _resident(shape):
    zeros = (0,) * len(shape)
    return pl.BlockSpec(shape, lambda *_: zeros, pipeline_mode=pl.Buffered(1))


def _mod_spec(arr):
    if arr.shape[0] == 1:
        return pl.BlockSpec((None, 1, D_MODEL), lambda b, i: (0, 0, 0))
    return pl.BlockSpec((None, 1, D_MODEL), lambda b, i: (b, 0, 0))


def _tok_spec(tm, width=D_MODEL):
    return pl.BlockSpec((None, tm, width), lambda b, i: (b, i, 0))


def _params(*sem):
    return pltpu.CompilerParams(dimension_semantics=sem, vmem_limit_bytes=VMEM_LIMIT)


def _ada_kernel(c_ref, w_ref, b_ref, o_ref):
    s = _silu(c_ref[...]).astype(BF16)
    o_ref[...] = _dot(s, w_ref[...].astype(BF16)) + b_ref[...]


def _ada(cvec, ada_w, ada_b):
    depth, _, n = ada_w.shape
    tn = 1536
    return pl.pallas_call(
        _ada_kernel,
        grid=(depth, n // tn),
        in_specs=[pl.BlockSpec((8, D_MODEL), lambda l, j: (0, 0)),
                  pl.BlockSpec((None, D_MODEL, tn), lambda l, j: (l, 0, j)),
                  pl.BlockSpec((None, 1, tn), lambda l, j: (l, 0, j))],
        out_specs=pl.BlockSpec((None, 8, tn), lambda l, j: (l, 0, j)),
        out_shape=jax.ShapeDtypeStruct((depth, 8, n), F32),
        compiler_params=_params("arbitrary", "arbitrary"),
        name="ada",
    )(cvec, ada_w, ada_b.reshape(depth, 1, n))


def _qkv_kernel(x_ref, g_ref, sc_ref, sh_ref, w_ref, wvt_ref, gq_ref, gk_ref, red_ref, exp_ref,
                q_ref, k_ref, vt_ref):
    h = _norm_mod(x_ref[...], g_ref[...], sc_ref[...], sh_ref[...]).astype(BF16)

    def head_norm(z, gain, out_ref):
        ss = _dot((z * z).astype(BF16), red_ref[...])
        r = lax.rsqrt(ss * (1.0 / NA_HEAD_DIM) + EPS)
        rex = _dot(jnp.concatenate(_split_bf16(r), axis=1), exp_ref[...])
        zn = (z * rex * gain).astype(BF16)
        for g in range(D_MODEL // GROUP_W):
            out_ref[g] = zn[:, g * GROUP_W:(g + 1) * GROUP_W]

    q = _dot(h, w_ref[:, 0:D_MODEL])
    k = _dot(h, w_ref[:, D_MODEL:2 * D_MODEL])
    head_norm(q, gq_ref[...], q_ref)
    vt = _dot_nt(wvt_ref[...], h).astype(BF16)
    head_norm(k, gk_ref[...], k_ref)
    for j in range(vt_ref.shape[0]):
        vt_ref[j] = vt[:, j * Q_BLK:(j + 1) * Q_BLK]


def _qkv(x, g, sc, sh, w_qk, w_vt, gq, gk, red, expand, tm):
    B, T, _ = x.shape
    n_grp = D_MODEL // GROUP_W
    grp = jax.ShapeDtypeStruct((n_grp, B, T, GROUP_W), BF16)
    grp_spec = pl.BlockSpec((n_grp, None, tm, GROUP_W), lambda b, i: (0, b, i, 0))
    return pl.pallas_call(
        _qkv_kernel,
        grid=(B, T // tm),
        in_specs=[_tok_spec(tm), _resident((1, D_MODEL)), _mod_spec(sc), _mod_spec(sh),
                  _resident(w_qk.shape), _resident(w_vt.shape), _resident((1, D_MODEL)), _resident((1, D_MODEL)),
                  _resident(red.shape), _resident(expand.shape)],
        out_specs=[grp_spec, grp_spec,
                   pl.BlockSpec((None, tm // Q_BLK, D_MODEL, Q_BLK), lambda b, i: (b, i, 0, 0))],
        out_shape=[grp, grp, jax.ShapeDtypeStruct((B, T // Q_BLK, D_MODEL, Q_BLK), BF16)],
        compiler_params=_params("parallel", "parallel"),
        name="qkv",
    )(x, g, sc, sh, w_qk, w_vt, gq, gk, red, expand)


def _attend(tasks):
    def stage_scores(task):
        q_h, keys, _, biases = task
        s = [_dot_nt(key(), q_h) + bias() for key, bias in zip(keys, biases)]
        m = functools.reduce(jnp.maximum, [jnp.max(t, axis=0, keepdims=True) for t in s])
        return s, m

    def stage_probs(sm):
        s, m = sm
        return [jnp.exp2(t - m).astype(BF16) for t in s]

    n = len(tasks)
    scores, probs, outs = {}, {}, []
    for step in range(n + 2):
        if step < n:
            scores[step] = stage_scores(tasks[step])
        if 0 <= step - 1 < n:
            probs[step - 1] = stage_probs(scores.pop(step - 1))
        if 0 <= step - 2 < n:
            outs.append(_weighted_values(tasks[step - 2], probs.pop(step - 2)))
    return outs


def _attend_bounded(tasks):
    def stage_probs(task):
        q_h, keys, _, biases = task
        return [jnp.exp2(_dot_nt(key(), q_h) + bias()).astype(BF16) for key, bias in zip(keys, biases)]

    n = len(tasks)
    probs, outs = {}, []
    for step in range(n + 1):
        if step < n:
            probs[step] = stage_probs(tasks[step])
        if step >= 1:
            outs.append(_weighted_values(tasks[step - 1], probs.pop(step - 1)))
    return outs


def _weighted_values(task, p):
    ones_rows = 16
    acc = None
    for vt, t in zip(task[2], p):
        v1 = jnp.concatenate([vt(), jnp.ones((ones_rows, t.shape[0]), BF16)], axis=0)
        part = _dot(v1, t)
        acc = part if acc is None else acc + part
    return acc[:NA_HEAD_DIM] * (1.0 / acc[NA_HEAD_DIM:NA_HEAD_DIM + 1])


def _head_queries(q_grp):
    lane_head = lax.broadcasted_iota(jnp.int32, (1, q_grp.shape[1]), 1) // NA_HEAD_DIM
    return [jnp.where(lane_head == h, q_grp, jnp.zeros_like(q_grp)) for h in range(ATTN_GROUP)]


def _head_rows(ref, h):
    return lambda: ref[h * NA_HEAD_DIM:(h + 1) * NA_HEAD_DIM, :]


def _attn_kernel(shift_ref, q_ref, k0_ref, k1_ref, k2_ref, k3_ref, v0_ref, v1_ref, v2_ref, v3_ref, kc_ref, vc_ref,
                 bias0_ref, bias1_ref, o_ref):
    k_refs = (k0_ref, k1_ref, k2_ref, k3_ref)
    v_refs = (v0_ref, v1_ref, v2_ref, v3_ref)
    bias_refs = (bias0_ref, bias1_ref)
    n_ctx = kc_ref.shape[0] // Q_BLK
    shift = shift_ref[0]
    tasks = []
    for jb in range(Q_BLOCKS_PER_STEP):
        q_heads = _head_queries(q_ref[jb * Q_BLK:(jb + 1) * Q_BLK, :])
        for h in range(ATTN_GROUP):
            keys = ([(lambda r=r: r[...]) for r in k_refs[jb:jb + KEY_SLOTS]]
                    + [(lambda t=t: kc_ref[t * Q_BLK:(t + 1) * Q_BLK, :]) for t in range(n_ctx)])
            vts = ([_head_rows(r, h) for r in v_refs[jb:jb + KEY_SLOTS]]
                   + [_head_rows(vc_ref.at[t], h) for t in range(n_ctx)])
            biases = ([(lambda t=t, jb=jb, h=h: bias_refs[jb][h, t * Q_BLK:(t + 1) * Q_BLK, :])
                       for t in range(KEY_SLOTS)] + [lambda: shift] * n_ctx)
            tasks.append((q_heads[h], keys, vts, biases))

    def finish(outs):
        for jb in range(Q_BLOCKS_PER_STEP):
            o_t = jnp.concatenate(outs[jb * ATTN_GROUP:(jb + 1) * ATTN_GROUP], axis=0)
            o_ref[jb * Q_BLK:(jb + 1) * Q_BLK, :] = o_t.T.astype(BF16)

    bounded = shift_ref[1] > 0.5

    @pl.when(bounded)
    def _():
        finish(_attend_bounded(tasks))

    @pl.when(jnp.logical_not(bounded))
    def _():
        finish(_attend(tasks))


def _attn(shift, q, k, vt, kc, vtc, bias):
    n_grp, B, T, gw = q.shape
    L = kc.shape[2]
    nb = T // Q_BLK
    per = Q_BLOCKS_PER_STEP
    n_halo = per + KEY_SLOTS - 1

    def blk(i, d):
        return jnp.clip(per * i + d, 0, nb - 1)

    def variant(qb):
        return jnp.where(qb == 0, 0, jnp.where(qb == nb - 1, 2, 1))

    halo = [pl.BlockSpec((None, None, Q_BLK, gw), lambda g, b, i, d=d: (g, b, blk(i, d - 1), 0))
            for d in range(n_halo)]
    halo_t = [pl.BlockSpec((None, None, gw, Q_BLK), lambda g, b, i, d=d: (b, blk(i, d - 1), g, 0))
              for d in range(n_halo)]
    biases = [pl.BlockSpec((None, ATTN_GROUP, KEY_SLOTS * Q_BLK, Q_BLK),
                           lambda g, b, i, jb=jb: (variant(per * i + jb), g, 0, 0))
              for jb in range(per)]
    tok = pl.BlockSpec((None, None, per * Q_BLK, gw), lambda g, b, i: (g, b, i, 0))
    return pl.pallas_call(
        _attn_kernel,
        grid=(n_grp, B, nb // per),
        in_specs=[pl.BlockSpec(memory_space=pltpu.SMEM), tok] + halo + halo_t
                 + [pl.BlockSpec((None, None, L, gw), lambda g, b, i: (g, b, 0, 0)),
                    pl.BlockSpec((None, L // Q_BLK, gw, Q_BLK), lambda g, b, i: (b, 0, g, 0))] + biases,
        out_specs=tok,
        out_shape=jax.ShapeDtypeStruct((n_grp, B, T, gw), BF16),
        compiler_params=_params("parallel", "parallel", "parallel"),
        name="attn",
    )(shift, q, *([k] * n_halo), *([vt] * n_halo), kc, vtc, *([bias] * per))


def _attn_ctx_kernel(q_ref, k_ref, vt_ref, o_ref):
    q_heads = _head_queries(q_ref[...])
    n_ctx = k_ref.shape[0] // Q_BLK
    tasks = [(q_heads[h],
              [(lambda t=t: k_ref[t * Q_BLK:(t + 1) * Q_BLK, :]) for t in range(n_ctx)],
              [_head_rows(vt_ref.at[t], h) for t in range(n_ctx)],
              [lambda: 0.0] * n_ctx) for h in range(ATTN_GROUP)]
    o_ref[...] = jnp.concatenate(_attend(tasks), axis=0).T.astype(BF16)


def _attn_ctx(q, k, vt):
    n_grp, B, L, gw = q.shape
    spec = pl.BlockSpec((None, None, L, gw), lambda b, g: (g, b, 0, 0))
    return pl.pallas_call(
        _attn_ctx_kernel,
        grid=(B, n_grp),
        in_specs=[spec, spec, pl.BlockSpec((None, L // Q_BLK, gw, Q_BLK), lambda b, g: (b, 0, g, 0))],
        out_specs=spec,
        out_shape=jax.ShapeDtypeStruct((n_grp, B, L, gw), BF16),
        compiler_params=_params("parallel", "parallel"),
        name="attn_ctx",
    )(q, k, vt)


def _attn_bias(rpb, rows, shift):
    n_heads = rpb.shape[0]
    n_slot = KEY_SLOTS * Q_ROWS
    cc = np.arange(GRID_W)[:, None]
    c = np.arange(GRID_W)[None, :]
    onehot = (cc - c + WIN_W - 1)[None] == np.arange(2 * WIN_W - 1)[:, None, None]
    toep = jnp.einsum('hdk,kxc->hdxc', rpb, jnp.asarray(onehot, F32), precision=lax.Precision.HIGHEST)
    lo_d = WIN_H - 1 - Q_ROWS
    vals = jnp.stack([toep[:, lo_d - a:lo_d - a + n_slot] for a in range(Q_ROWS)], axis=3)
    col_start = np.clip(c - WIN_W // 2, 0, GRID_W - WIN_W)
    col_ok = ((cc >= col_start) & (cc < col_start + WIN_W))[None, :, None, :]
    j = np.arange(n_slot)[:, None, None, None]
    a = np.arange(Q_ROWS)[None, None, :, None]
    kh = min(WIN_H, rows)
    lows = (np.full_like(a, Q_ROWS), a + Q_ROWS - kh // 2, np.full_like(a, 2 * Q_ROWS - kh))
    out = []
    for lo in lows:
        ok = np.broadcast_to((j >= lo) & (j < lo + kh) & col_ok, (n_slot, GRID_W, Q_ROWS, GRID_W))
        out.append(jnp.where(ok[None], vals + shift, MASKED).reshape(n_heads, KEY_SLOTS * Q_BLK, Q_BLK))
    return jnp.stack(out)


FF_CHUNK = D_FF // 2


def _mix_ffn_tail(mix, x_ref, g1_ref, n2_ref, sc2_ref, sh2_ref, g2_ref, wo_ref, win_ref, wout_ref, o_ref):
    x1 = x_ref[...] + g1_ref[...] * mix(wo_ref)
    h2 = _norm_mod(x1, n2_ref[...], sc2_ref[...], sh2_ref[...]).astype(BF16)
    acc = None
    for j in range(D_FF // FF_CHUNK):
        lo = j * FF_CHUNK
        gate = _dot(h2, win_ref[:, lo:lo + FF_CHUNK])
        up = _dot(h2, win_ref[:, D_FF + lo:D_FF + lo + FF_CHUNK])
        t = _dot((_silu(gate) * up).astype(BF16), wout_ref[lo:lo + FF_CHUNK, :])
        acc = t if acc is None else acc + t
    o_ref[...] = x1 + g2_ref[...] * acc


def _na_mix_ffn_kernel(a_ref, *rest):
    def mix(wo_ref):
        parts = [_dot(a_ref[g], wo_ref[g * GROUP_W:(g + 1) * GROUP_W, :]) for g in range(a_ref.shape[0])]
        return functools.reduce(jnp.add, parts)

    _mix_ffn_tail(mix, *rest)


def _hg_mix_ffn_kernel(of_ref, ob_ref, sg_ref, gn_ref, *rest):
    o = of_ref[...].astype(F32) + ob_ref[...].astype(F32)
    parts = []
    for h in range(HG_HEADS):
        oh = o[:, h * HG_DK:(h + 1) * HG_DK]
        ms = jnp.mean(oh * oh, axis=-1, keepdims=True)
        parts.append(oh * lax.rsqrt(ms + EPS))
    a = (jnp.concatenate(parts, axis=1) * gn_ref[...] * sg_ref[...].astype(F32)).astype(BF16)
    _mix_ffn_tail(lambda wo_ref: _dot(a, wo_ref[...]), *rest)


def _mix_ffn(kernel, name, lead, lead_specs, x, g1, n2, sc2, sh2, g2, wo, win, wout, tm):
    B, T, _ = x.shape
    return pl.pallas_call(
        kernel,
        grid=(B, T // tm),
        in_specs=list(lead_specs) + [
            _tok_spec(tm), _mod_spec(g1), _resident((1, D_MODEL)), _mod_spec(sc2), _mod_spec(sh2), _mod_spec(g2),
            _resident(wo.shape), _resident(win.shape), _resident(wout.shape)],
        out_specs=_tok_spec(tm),
        out_shape=jax.ShapeDtypeStruct((B, T, D_MODEL), F32),
        compiler_params=_params("parallel", "parallel"),
        name=name,
    )(*lead, x, g1, n2, sc2, sh2, g2, wo, win, wout)


def _hg_proj_kernel(x_ref, g_ref, sc_ref, sh_ref, w_ref, lb_ref, tri3_ref, q_ref, v_ref, sg_ref, kk_ref, cum_ref):
    h = _norm_mod(x_ref[...], g_ref[...], sc_ref[...], sh_ref[...]).astype(BF16)

    def project(n):
        return _dot(h, w_ref[:, n * D_MODEL:(n + 1) * D_MODEL])

    def gates(d, f_pre):
        lb = lb_ref[d]
        half = 0.5 * (1.0 - lb)
        th = jnp.tanh(0.5 * f_pre)
        f = jnp.maximum(0.5 * (1.0 + lb) + half * th, GATE_FLOOR)
        kk_ref[d] = (half - half * th).astype(BF16)
        lf = jnp.log2(f)
        hi = lf.astype(BF16)
        mid, lo = _split_bf16(lf - hi.astype(F32))
        for c in range(f_pre.shape[0] // SCAN_CHUNK):
            rows = slice(c * SCAN_CHUNK, (c + 1) * SCAN_CHUNK)
            parts = jnp.concatenate([hi[rows], mid[rows], lo[rows]], axis=0)
            cum_ref[d, rows, :] = _dot(tri3_ref[d], parts)

    tails = [lambda z: q_ref.__setitem__(Ellipsis, _silu(z).astype(BF16)),
             lambda z: v_ref.__setitem__(Ellipsis, z.astype(BF16)),
             lambda z: sg_ref.__setitem__(Ellipsis, _silu(z).astype(BF16)),
             functools.partial(gates, 0), functools.partial(gates, 1)]
    pending = project(0)
    for n, tail in enumerate(tails):
        z = pending
        if n + 1 < len(tails):
            pending = project(n + 1)
        tail(z)


def _hg_proj(x, g, sc, sh, w, lb, tri3, tm):
    B, T, _ = x.shape
    tok = jax.ShapeDtypeStruct((B, T, D_MODEL), BF16)
    two = pl.BlockSpec((2, None, tm, D_MODEL), lambda b, i: (0, b, i, 0))
    return pl.pallas_call(
        _hg_proj_kernel,
        grid=(B, T // tm),
        in_specs=[_tok_spec(tm), _resident((1, D_MODEL)), _mod_spec(sc), _mod_spec(sh),
                  _resident(w.shape), _resident(lb.shape), _resident(tri3.shape)],
        out_specs=[_tok_spec(tm)] * 3 + [two, two],
        out_shape=[tok] * 3 + [jax.ShapeDtypeStruct((2, B, T, D_MODEL), BF16),
                               jax.ShapeDtypeStruct((2, B, T, D_MODEL), F32)],
        compiler_params=_params("parallel", "parallel"),
        name="hg_proj",
    )(x, g, sc, sh, w, lb, tri3)


def _hg_scan_kernel(qf_ref, vf_ref, kkf_ref, cumf_ref, qb_ref, vb_ref, kkb_ref, cumb_ref, tri_ref, s0_ref,
                    of_ref, ob_ref, st_ref, a_scr):
    @pl.when(pl.program_id(1) == 0)
    def _():
        st_ref[...] = s0_ref[...]

    C = SCAN_CHUNK
    n_sub = qf_ref.shape[0] // C
    dirs = ((qf_ref, vf_ref, kkf_ref, cumf_ref, of_ref), (qb_ref, vb_ref, kkb_ref, cumb_ref, ob_ref))
    streams = [(d, slice((j if d == 0 else n_sub - 1 - j) * C, (j if d == 0 else n_sub - 1 - j) * C + C))
               for j in range(n_sub) for d in range(2)]
    pair_w = 2 * HG_DK
    pairs = [slice(p * pair_w, (p + 1) * pair_w) for p in range(HG_HEADS // 2)]
    first = lax.broadcasted_iota(jnp.int32, (1, pair_w), 1) < HG_DK

    def block_diag(x):
        zero = jnp.zeros_like(x)
        return jnp.concatenate([jnp.where(first, x, zero), jnp.where(first, zero, x)], axis=0)

    def chunk_total(d, rows):
        last = rows.stop - 1 if d == 0 else rows.start
        return dirs[d][3][last:last + 1, :]

    out_of_range = [jnp.max(-0.5 * chunk_total(d, rows)) > FACTOR_RANGE for d, rows in streams]

    def stage_prep(n):
        d, rows = streams[n]
        q_ref, _, kk_ref, cum_ref, _ = dirs[d]
        b = cum_ref[rows, :]
        tot = chunk_total(d, rows)
        mid = 0.5 * tot
        q = q_ref[rows, :]
        kk = kk_ref[rows, :]
        return dict(q_in=q * jnp.exp2(b - mid).astype(BF16), k_in=kk * jnp.exp2(mid - b).astype(BF16),
                    q_st=q * jnp.exp2(b).astype(BF16), k_st=kk * jnp.exp2(tot - b).astype(BF16),
                    decay=jnp.exp2(tot))

    def stage_intra(n, p, check_range):
        d, rows = streams[n]
        seen = jnp.concatenate([tri_ref[d]] * 2, axis=1) > 0.0
        a_s = [jnp.where(seen, _dot_nt(p["q_in"][:, pr], block_diag(p["k_in"][:, pr])), 0.0).astype(BF16)
               for pr in pairs]
        if not check_range:
            return a_s
        a_scr[n] = jnp.concatenate(a_s, axis=1)

        @pl.when(out_of_range[n])
        def _():
            a_scr[n] = _pairwise_intra(dirs[d][0][rows, :].astype(F32), dirs[d][2][rows, :].astype(F32),
                                       dirs[d][3], rows, d == 0).astype(BF16)

        return [a_scr[n, :, i * 2 * C:(i + 1) * 2 * C] for i in range(len(pairs))]

    def stage_values(n, p, a_s):
        d, rows = streams[n]
        v_ref = dirs[d][1]
        o_intra = [_dot(a, block_diag(v_ref[rows, pr])) for a, pr in zip(a_s, pairs)]
        update = [_dot_tn(v_ref[rows, h * HG_DK:(h + 1) * HG_DK], p["k_st"][:, h * HG_DK:(h + 1) * HG_DK])
                  for h in range(HG_HEADS)]
        return o_intra, update

    def stage_state(n, p, o_intra, update):
        d, rows = streams[n]
        zero = jnp.zeros((HG_DK, HG_DK), BF16)
        outs = []
        for i, pr in enumerate(pairs):
            st_a, st_b = st_ref[d, 2 * i], st_ref[d, 2 * i + 1]
            st_bd = jnp.concatenate([jnp.concatenate([st_a.astype(BF16), zero], axis=1),
                                     jnp.concatenate([zero, st_b.astype(BF16)], axis=1)], axis=0)
            outs.append(o_intra[i] + _dot_nt(p["q_st"][:, pr], st_bd))
            st_ref[d, 2 * i] = st_a * p["decay"][:, pr][:, :HG_DK] + update[2 * i]
            st_ref[d, 2 * i + 1] = st_b * p["decay"][:, pr][:, HG_DK:] + update[2 * i + 1]
        dirs[d][4][rows, :] = jnp.concatenate(outs, axis=1).astype(BF16)

    def run(check_range):
        n_str = len(streams)
        prep, intra, vals = {}, {}, {}
        for t in range(n_str + 3):
            if t < n_str:
                prep[t] = stage_prep(t)
            if 0 <= t - 1 < n_str:
                intra[t - 1] = stage_intra(t - 1, prep[t - 1], check_range)
            if 0 <= t - 2 < n_str:
                vals[t - 2] = stage_values(t - 2, prep[t - 2], intra.pop(t - 2))
            if 0 <= t - 3 < n_str:
                stage_state(t - 3, prep.pop(t - 3), *vals.pop(t - 3))

    any_out = functools.reduce(jnp.logical_or, out_of_range)
    pl.when(jnp.logical_not(any_out))(lambda: run(False))
    pl.when(any_out)(lambda: run(True))


def _pairwise_intra(q, kk, cum_ref, rows, forward):
    C = q.shape[0]
    b = cum_ref[rows, :]
    t_idx = lax.broadcasted_iota(jnp.int32, (C, 1), 0)
    lane = lax.broadcasted_iota(jnp.int32, (1, HG_HEADS * C), 1)

    def body(s, acc):
        b_s = cum_ref[pl.ds(rows.start + s, 1), :]
        k_s = jnp.sum(jnp.where(t_idx == s, kk, 0.0), axis=0, keepdims=True)
        w = q * (k_s * jnp.exp2(jnp.minimum(b - b_s, 0.0)))
        seen = (t_idx >= s) if forward else (t_idx <= s)
        for h in range(HG_HEADS):
            col = jnp.sum(w[:, h * HG_DK:(h + 1) * HG_DK], axis=1, keepdims=True)
            acc = jnp.where(lane == h * C + s, jnp.where(seen, col, 0.0), acc)
        return acc

    return lax.fori_loop(0, C, body, jnp.zeros((C, HG_HEADS * C), F32))


def _hg_scan(q, v, kk, cum, tri, s0):
    B, T, _ = q.shape
    C = SCAN_CHUNK * (SCAN_SUB if T % (SCAN_CHUNK * SCAN_SUB) == 0 else 1)
    nc = T // C
    fwd = pl.BlockSpec((None, C, D_MODEL), lambda b, i: (b, i, 0))
    bwd = pl.BlockSpec((None, C, D_MODEL), lambda b, i: (b, nc - 1 - i, 0))
    dfwd = pl.BlockSpec((None, None, C, D_MODEL), lambda b, i: (0, b, i, 0))
    dbwd = pl.BlockSpec((None, None, C, D_MODEL), lambda b, i: (1, b, nc - 1 - i, 0))
    state = pl.BlockSpec((2, None, HG_HEADS, HG_DK, HG_DK), lambda b, i: (0, b, 0, 0, 0))
    tok = jax.ShapeDtypeStruct((B, T, D_MODEL), BF16)
    return pl.pallas_call(
        _hg_scan_kernel,
        grid=(B, nc),
        in_specs=[fwd, fwd, dfwd, dfwd, bwd, bwd, dbwd, dbwd, _resident(tri.shape), state],
        out_specs=[fwd, bwd, state],
        out_shape=[tok, tok, jax.ShapeDtypeStruct((2, B, HG_HEADS, HG_DK, HG_DK), F32)],
        scratch_shapes=[pltpu.VMEM((2 * C // SCAN_CHUNK, SCAN_CHUNK, HG_HEADS * SCAN_CHUNK), BF16)],
        compiler_params=_params("parallel", "arbitrary"),
        name="hg_scan",
    )(q, v, kk, cum, q, v, kk, cum, tri, s0)


def _tile(n, pref):
    return pref if n % pref == 0 else n


def kernel(x, c, ctx, c_ctx, ada_w, ada_b, norm1_g, norm2_g, na_w_qkv, na_w_o, na_q_gain, na_k_gain, na_rpb,
           hg_w_in, hg_lower, hg_norm_g, hg_w_o, ffn_w_in, ffn_w_out):
    B, T, _ = x.shape
    L = ctx.shape[1]
    rows = T // GRID_W
    assert B + 1 <= 8 and T % (Q_BLOCKS_PER_STEP * Q_BLK) == 0 and rows >= WIN_H
    assert T % SCAN_CHUNK == 0 and L % Q_BLK == 0
    tm = _tile(T, 512)
    tl = _tile(L, 256)
    row = lambda a: a.reshape(1, -1).astype(F32)

    cvec = jnp.zeros((8, D_MODEL), F32).at[:B].set(c).at[B].set(c_ctx)
    mods = _ada(cvec, ada_w, ada_b)

    def mod_rows(i):
        lat = [m.reshape(B, 1, D_MODEL) for m in jnp.split(mods[i, :B], N_MOD, axis=-1)]
        cx = [m.reshape(1, 1, D_MODEL) for m in jnp.split(mods[i, B], N_MOD, axis=-1)]
        return lat, cx

    (sh1, sc1, g1, sh2, sc2, g2), (csh1, csc1, cg1, csh2, csc2, cg2) = mod_rows(0)
    w_qk = na_w_qkv[0, :, :2 * D_MODEL].astype(BF16)
    w_vt = na_w_qkv[0, :, 2 * D_MODEL:].T.astype(BF16)
    gq = row(jnp.tile(na_q_gain[0], NA_HEADS) * (NA_HEAD_DIM ** -0.5 * LOG2E))
    gk = row(jnp.tile(na_k_gain[0], NA_HEADS))
    head_of = np.arange(D_MODEL) // NA_HEAD_DIM
    red = jnp.asarray(head_of[:, None] == np.arange(LANES)[None, :], BF16)
    expand = jnp.asarray(np.arange(2 * LANES)[:, None] % LANES == head_of[None, :], BF16)
    n1 = row(norm1_g[0])
    q_l, k_l, vt_l = _qkv(x, n1, sc1, sh1, w_qk, w_vt, gq, gk, red, expand, tm)
    q_c, k_c, vt_c = _qkv(ctx, n1, csc1, csh1, w_qk, w_vt, gq, gk, red, expand, tl)
    rpb2 = na_rpb[0].astype(F32) * LOG2E
    qk_max = NA_HEAD_DIM * jnp.max(jnp.abs(gq)) * jnp.max(jnp.abs(gk))
    bound = qk_max + jnp.maximum(jnp.max(rpb2), 0.0)
    shift = jnp.stack([-bound, (bound + qk_max < BOUNDED_SPREAD).astype(F32)])
    a_l = _attn(shift, q_l, k_l, vt_l, k_c, vt_c, _attn_bias(rpb2, rows, -bound))
    a_c = _attn_ctx(q_c, k_c, vt_c)
    ffn = (na_w_o[0].astype(BF16), ffn_w_in[0].astype(BF16), ffn_w_out[0].astype(BF16))
    n2 = row(norm2_g[0])
    grp_spec = lambda t: pl.BlockSpec((D_MODEL // GROUP_W, None, t, GROUP_W), lambda b, i: (0, b, i, 0))
    x_lat = _mix_ffn(_na_mix_ffn_kernel, "na_mix_ffn", (a_l,), (grp_spec(tm),),
                     x, g1, n2, sc2, sh2, g2, *ffn, tm)
    x_ctx = _mix_ffn(_na_mix_ffn_kernel, "na_mix_ffn_ctx", (a_c,), (grp_spec(tl),),
                     ctx, cg1, n2, csc2, csh2, cg2, *ffn, tl)

    (sh1, sc1, g1, sh2, sc2, g2), (csh1, csc1, _, _, _, _) = mod_rows(1)
    lbs = jnp.cumsum(jax.nn.softmax(hg_lower.astype(F32), axis=0), axis=0)
    lb = (lbs - lbs[:1])[1].reshape(2, 1, D_MODEL)
    w_in = hg_w_in[0].astype(BF16)
    n1 = row(norm1_g[1])
    t_idx = np.arange(SCAN_CHUNK)
    tri_np = np.stack([t_idx[None, :] <= t_idx[:, None], t_idx[None, :] >= t_idx[:, None]])
    tri = jnp.asarray(tri_np, F32)
    tri3 = jnp.asarray(np.tile(tri_np, (1, 1, 3)), BF16)
    q_c, v_c, _, kk_c, cum_c = _hg_proj(x_ctx, n1, csc1, csh1, w_in, lb, tri3, tl)
    q_l, v_l, sg_l, kk_l, cum_l = _hg_proj(x_lat, n1, sc1, sh1, w_in, lb, tri3, tm)
    zero = jnp.zeros((2, B, HG_HEADS, HG_DK, HG_DK), F32)
    _, _, s_ctx = _hg_scan(q_c, v_c, kk_c, cum_c, tri, zero)
    o_f, o_b, _ = _hg_scan(q_l, v_l, kk_l, cum_l, tri, s_ctx)
    ffn = (hg_w_o[0].astype(BF16), ffn_w_in[1].astype(BF16), ffn_w_out[1].astype(BF16))
    gn = row(jnp.tile(hg_norm_g[0], HG_HEADS))
    return _mix_ffn(_hg_mix_ffn_kernel, "hg_mix_ffn", (o_f, o_b, sg_l, gn),
                    (_tok_spec(tm), _tok_spec(tm), _tok_spec(tm), _resident((1, D_MODEL))),
                    x_lat, g1, row(norm2_g[1]), sc2, sh2, g2, *ffn, tm)
```

```python
import functools

import jax
import jax.numpy as jnp
import numpy as np
from jax import lax
from jax.experimental import pallas as pl
from jax.experimental.pallas import tpu as pltpu

F32 = jnp.float32
BF16 = jnp.bfloat16

D_MODEL = 1024
EPS = 1e-6
GRID_W = 64
WIN_H = 8
WIN_W = 16
NA_HEADS = 16
NA_HEAD_DIM = D_MODEL // NA_HEADS
HG_HEADS = 8
HG_DK = 128
D_FF = 2816
N_MOD = 6

LANES = 128
Q_ROWS = 4
Q_BLK = Q_ROWS * GRID_W
KEY_SLOTS = 3
ATTN_GROUP = 4
GROUP_W = ATTN_GROUP * NA_HEAD_DIM
BOUNDED_SPREAD = 100.0
Q_BLOCKS_PER_STEP = 4
LOG2E = 1.4426950408889634
MASKED = -1e30
SCAN_CHUNK = 64
SCAN_SUB = 8
FACTOR_RANGE = 100.0
GATE_FLOOR = 2.0 ** -100
VMEM_LIMIT = 56 * 1024 * 1024


def _dot(a, b):
    return jnp.dot(a, b, preferred_element_type=F32)


def _dot_nt(a, b):
    return lax.dot_general(a, b, (((1,), (1,)), ((), ())), preferred_element_type=F32)


def _dot_tn(a, b):
    return lax.dot_general(a, b, (((0,), (0,)), ((), ())), preferred_element_type=F32)


def _silu(x):
    half = 0.5 * x
    return half + half * jnp.tanh(half)


def _norm_mod(xf, g, sc, sh):
    ms = jnp.mean(xf * xf, axis=-1, keepdims=True)
    return (xf * lax.rsqrt(ms + EPS) * g) * (1.0 + sc) + sh


def _split_bf16(x):
    hi = x.astype(BF16)
    lo = (x - hi.astype(F32)).astype(BF16)
    return hi, lo


def _resident(shape):
    zeros = (0,) * len(shape)
    return pl.BlockSpec(shape, lambda *_: zeros, pipeline_mode=pl.Buffered(1))


def _mod_spec(arr):
    if arr.shape[0] == 1:
        return pl.BlockSpec((None, 1, D_MODEL), lambda b, i: (0, 0, 0))
    return pl.BlockSpec((None, 1, D_MODEL), lambda b, i: (b, 0, 0))


def _tok_spec(tm, width=D_MODEL):
    return pl.BlockSpec((None, tm, width), lambda b, i: (b, i, 0))


def _params(*sem):
    return pltpu.CompilerParams(dimension_semantics=sem, vmem_limit_bytes=VMEM_LIMIT)


def _ada_kernel(c_ref, w_ref, b_ref, o_ref):
    s = _silu(c_ref[...]).astype(BF16)
    o_ref[...] = _dot(s, w_ref[...].astype(BF16)) + b_ref[...]


def _ada(cvec, ada_w, ada_b):
    depth, _, n = ada_w.shape
    tn = 1536
    return pl.pallas_call(
        _ada_kernel,
        grid=(depth, n // tn),
        in_specs=[pl.BlockSpec((8, D_MODEL), lambda l, j: (0, 0)),
                  pl.BlockSpec((None, D_MODEL, tn), lambda l, j: (l, 0, j)),
                  pl.BlockSpec((None, 1, tn), lambda l, j: (l, 0, j))],
        out_specs=pl.BlockSpec((None, 8, tn), lambda l, j: (l, 0, j)),
        out_shape=jax.ShapeDtypeStruct((depth, 8, n), F32),
        compiler_params=_params("arbitrary", "arbitrary"),
        name="ada",
    )(cvec, ada_w, ada_b.reshape(depth, 1, n))


def _qkv_kernel(x_ref, g_ref, sc_ref, sh_ref, w_ref, wvt_ref, gq_ref, gk_ref, red_ref, exp_ref,
                q_ref, k_ref, vt_ref):
    h = _norm_mod(x_ref[...], g_ref[...], sc_ref[...], sh_ref[...]).astype(BF16)

    def head_norm(z, gain, out_ref):
        ss = _dot((z * z).astype(BF16), red_ref[...])
        r = lax.rsqrt(ss * (1.0 / NA_HEAD_DIM) + EPS)
        rex = _dot(jnp.concatenate(_split_bf16(r), axis=1), exp_ref[...])
        zn = (z * rex * gain).astype(BF16)
        for g in range(D_MODEL // GROUP_W):
            out_ref[g] = zn[:, g * GROUP_W:(g + 1) * GROUP_W]

    q = _dot(h, w_ref[:, 0:D_MODEL])
    k = _dot(h, w_ref[:, D_MODEL:2 * D_MODEL])
    head_norm(q, gq_ref[...], q_ref)
    vt = _dot_nt(wvt_ref[...], h).astype(BF16)
    head_norm(k, gk_ref[...], k_ref)
    for j in range(vt_ref.shape[0]):
        vt_ref[j] = vt[:, j * Q_BLK:(j + 1) * Q_BLK]


def _qkv(x, g, sc, sh, w_qk, w_vt, gq, gk, red, expand, tm):
    B, T, _ = x.shape
    n_grp = D_MODEL // GROUP_W
    grp = jax.ShapeDtypeStruct((n_grp, B, T, GROUP_W), BF16)
    grp_spec = pl.BlockSpec((n_grp, None, tm, GROUP_W), lambda b, i: (0, b, i, 0))
    return pl.pallas_call(
        _qkv_kernel,
        grid=(B, T // tm),
        in_specs=[_tok_spec(tm), _resident((1, D_MODEL)), _mod_spec(sc), _mod_spec(sh),
                  _resident(w_qk.shape), _resident(w_vt.shape), _resident((1, D_MODEL)), _resident((1, D_MODEL)),
                  _resident(red.shape), _resident(expand.shape)],
        out_specs=[grp_spec, grp_spec,
                   pl.BlockSpec((None, tm // Q_BLK, D_MODEL, Q_BLK), lambda b, i: (b, i, 0, 0))],
        out_shape=[grp, grp, jax.ShapeDtypeStruct((B, T // Q_BLK, D_MODEL, Q_BLK), BF16)],
        compiler_params=_params("parallel", "parallel"),
        name="qkv",
    )(x, g, sc, sh, w_qk, w_vt, gq, gk, red, expand)


def _attend(tasks):
    def stage_scores(task):
        q_h, keys, _, biases = task
        s = [_dot_nt(key(), q_h) + bias() for key, bias in zip(keys, biases)]
        m = functools.reduce(jnp.maximum, [jnp.max(t, axis=0, keepdims=True) for t in s])
        return s, m

    def stage_probs(sm):
        s, m = sm
        return [jnp.exp2(t - m).astype(BF16) for t in s]

    n = len(tasks)
    scores, probs, outs = {}, {}, []
    for step in range(n + 2):
        if step < n:
            scores[step] = stage_scores(tasks[step])
        if 0 <= step - 1 < n:
            probs[step - 1] = stage_probs(scores.pop(step - 1))
        if 0 <= step - 2 < n:
            outs.append(_weighted_values(tasks[step - 2], probs.pop(step - 2)))
    return outs


def _attend_bounded(tasks):
    def stage_probs(task):
        q_h, keys, _, biases = task
        return [jnp.exp2(_dot_nt(key(), q_h) + bias()).astype(BF16) for key, bias in zip(keys, biases)]

    n = len(tasks)
    probs, outs = {}, []
    for step in range(n + 1):
        if step < n:
            probs[step] = stage_probs(tasks[step])
        if step >= 1:
            outs.append(_weighted_values(tasks[step - 1], probs.pop(step - 1)))
    return outs


def _weighted_values(task, p):
    ones_rows = 16
    acc = None
    for vt, t in zip(task[2], p):
        v1 = jnp.concatenate([vt(), jnp.ones((ones_rows, t.shape[0]), BF16)], axis=0)
        part = _dot(v1, t)
        acc = part if acc is None else acc + part
    return acc[:NA_HEAD_DIM] * (1.0 / acc[NA_HEAD_DIM:NA_HEAD_DIM + 1])


def _head_queries(q_grp):
    lane_head = lax.broadcasted_iota(jnp.int32, (1, q_grp.shape[1]), 1) // NA_HEAD_DIM
    return [jnp.where(lane_head == h, q_grp, jnp.zeros_like(q_grp)) for h in range(ATTN_GROUP)]


def _head_rows(ref, h):
    return lambda: ref[h * NA_HEAD_DIM:(h + 1) * NA_HEAD_DIM, :]


def _attn_kernel(shift_ref, q_ref, *refs):
    n_halo = Q_BLOCKS_PER_STEP + KEY_SLOTS - 1
    k_refs, v_refs = refs[:n_halo], refs[n_halo:2 * n_halo]
    kc_ref, vc_ref = refs[2 * n_halo:2 * n_halo + 2]
    bias_first, bias_mid, bias_last, o_ref = refs[2 * n_halo + 2:]
    bias_refs = [bias_first] + [bias_mid] * (Q_BLOCKS_PER_STEP - 2) + [bias_last]
    n_ctx = kc_ref.shape[0] // Q_BLK
    shift = shift_ref[0]
    tasks = []
    for jb in range(Q_BLOCKS_PER_STEP):
        q_heads = _head_queries(q_ref[jb * Q_BLK:(jb + 1) * Q_BLK, :])
        for h in range(ATTN_GROUP):
            keys = ([(lambda r=r: r[...]) for r in k_refs[jb:jb + KEY_SLOTS]]
                    + [(lambda t=t: kc_ref[t * Q_BLK:(t + 1) * Q_BLK, :]) for t in range(n_ctx)])
            vts = ([_head_rows(r, h) for r in v_refs[jb:jb + KEY_SLOTS]]
                   + [_head_rows(vc_ref.at[t], h) for t in range(n_ctx)])
            biases = ([(lambda t=t, jb=jb, h=h: bias_refs[jb][h, t * Q_BLK:(t + 1) * Q_BLK, :])
                       for t in range(KEY_SLOTS)] + [lambda: shift] * n_ctx)
            tasks.append((q_heads[h], keys, vts, biases))

    def finish(outs):
        for jb in range(Q_BLOCKS_PER_STEP):
            o_t = jnp.concatenate(outs[jb * ATTN_GROUP:(jb + 1) * ATTN_GROUP], axis=0)
            o_ref[jb * Q_BLK:(jb + 1) * Q_BLK, :] = o_t.T.astype(BF16)

    bounded = shift_ref[1] > 0.5

    @pl.when(bounded)
    def _():
        finish(_attend_bounded(tasks))

    @pl.when(jnp.logical_not(bounded))
    def _():
        finish(_attend(tasks))


def _attn(shift, q, k, vt, kc, vtc, bias):
    n_grp, B, T, gw = q.shape
    L = kc.shape[2]
    nb = T // Q_BLK
    per = Q_BLOCKS_PER_STEP
    n_halo = per + KEY_SLOTS - 1

    def blk(i, d):
        return jnp.clip(per * i + d, 0, nb - 1)

    def variant(qb):
        return jnp.where(qb == 0, 0, jnp.where(qb == nb - 1, 2, 1))

    halo = [pl.BlockSpec((None, None, Q_BLK, gw), lambda g, b, i, d=d: (g, b, blk(i, d - 1), 0))
            for d in range(n_halo)]
    halo_t = [pl.BlockSpec((None, None, gw, Q_BLK), lambda g, b, i, d=d: (b, blk(i, d - 1), g, 0))
              for d in range(n_halo)]
    biases = [pl.BlockSpec((None, ATTN_GROUP, KEY_SLOTS * Q_BLK, Q_BLK), index)
              for index in (lambda g, b, i: (variant(per * i), g, 0, 0),
                            lambda g, b, i: (1, g, 0, 0),
                            lambda g, b, i: (variant(per * i + per - 1), g, 0, 0))]
    tok = pl.BlockSpec((None, None, per * Q_BLK, gw), lambda g, b, i: (g, b, i, 0))
    return pl.pallas_call(
        _attn_kernel,
        grid=(n_grp, B, nb // per),
        in_specs=[pl.BlockSpec(memory_space=pltpu.SMEM), tok] + halo + halo_t
                 + [pl.BlockSpec((None, None, L, gw), lambda g, b, i: (g, b, 0, 0)),
                    pl.BlockSpec((None, L // Q_BLK, gw, Q_BLK), lambda g, b, i: (b, 0, g, 0))] + biases,
        out_specs=tok,
        out_shape=jax.ShapeDtypeStruct((n_grp, B, T, gw), BF16),
        compiler_params=_params("parallel", "parallel", "parallel"),
        name="attn",
    )(shift, q, *([k] * n_halo), *([vt] * n_halo), kc, vtc, bias, bias, bias)


def _attn_ctx_kernel(q_ref, k_ref, vt_ref, o_ref):
    q_heads = _head_queries(q_ref[...])
    n_ctx = k_ref.shape[0] // Q_BLK
    tasks = [(q_heads[h],
              [(lambda t=t: k_ref[t * Q_BLK:(t + 1) * Q_BLK, :]) for t in range(n_ctx)],
              [_head_rows(vt_ref.at[t], h) for t in range(n_ctx)],
              [lambda: 0.0] * n_ctx) for h in range(ATTN_GROUP)]
    o_ref[...] = jnp.concatenate(_attend(tasks), axis=0).T.astype(BF16)


def _attn_ctx(q, k, vt):
    n_grp, B, L, gw = q.shape
    spec = pl.BlockSpec((None, None, L, gw), lambda b, g: (g, b, 0, 0))
    return pl.pallas_call(
        _attn_ctx_kernel,
        grid=(B, n_grp),
        in_specs=[spec, spec, pl.BlockSpec((None, L // Q_BLK, gw, Q_BLK), lambda b, g: (b, 0, g, 0))],
        out_specs=spec,
        out_shape=jax.ShapeDtypeStruct((n_grp, B, L, gw), BF16),
        compiler_params=_params("parallel", "parallel"),
        name="attn_ctx",
    )(q, k, vt)


def _attn_bias(rpb, rows, shift):
    n_heads = rpb.shape[0]
    n_slot = KEY_SLOTS * Q_ROWS
    cc = np.arange(GRID_W)[:, None]
    c = np.arange(GRID_W)[None, :]
    onehot = (cc - c + WIN_W - 1)[None] == np.arange(2 * WIN_W - 1)[:, None, None]
    toep = jnp.einsum('hdk,kxc->hdxc', rpb, jnp.asarray(onehot, F32), precision=lax.Precision.HIGHEST)
    lo_d = WIN_H - 1 - Q_ROWS
    vals = jnp.stack([toep[:, lo_d - a:lo_d - a + n_slot] for a in range(Q_ROWS)], axis=3)
    col_start = np.clip(c - WIN_W // 2, 0, GRID_W - WIN_W)
    col_ok = ((cc >= col_start) & (cc < col_start + WIN_W))[None, :, None, :]
    full = (n_slot, GRID_W, Q_ROWS, GRID_W)
    bias = jnp.where(np.broadcast_to(col_ok, full)[None], vals + shift, MASKED)
    j = np.arange(n_slot)[:, None, None, None]
    a = np.arange(Q_ROWS)[None, None, :, None]
    kh = min(WIN_H, rows)
    lows = (np.full_like(a, Q_ROWS), a + Q_ROWS - kh // 2, np.full_like(a, 2 * Q_ROWS - kh))
    row_mask = np.stack([np.where(np.broadcast_to((j >= lo) & (j < lo + kh), full), 0.0, MASKED) for lo in lows])
    return (bias.reshape(n_heads, KEY_SLOTS * Q_BLK, Q_BLK),
            jnp.asarray(row_mask.reshape(3, KEY_SLOTS * Q_BLK, Q_BLK), F32))


MXU_DEPTH = 256
FF_BOUNDS = (0, MXU_DEPTH * (D_FF // MXU_DEPTH + 1) // 2, D_FF)


def _mix_ffn_tail(mix, x_ref, g1_ref, n2_ref, sc2_ref, sh2_ref, g2_ref, wo_ref, win_ref, wout_ref, o_ref):
    tm = x_ref.shape[0]
    n_sub = 2 if tm % (2 * Q_BLK) == 0 else 1
    halves = [slice(i * (tm // n_sub), (i + 1) * (tm // n_sub)) for i in range(n_sub)]

    def head(rows):
        x1 = x_ref[rows, :] + g1_ref[...] * mix(wo_ref, rows)
        return x1, _norm_mod(x1, n2_ref[...], sc2_ref[...], sh2_ref[...]).astype(BF16)

    def ffn(rows, x1, h2):
        acc = None
        for lo, hi in zip(FF_BOUNDS[:-1], FF_BOUNDS[1:]):
            gate = _dot(h2, win_ref[:, lo:hi])
            up = _dot(h2, win_ref[:, D_FF + lo:D_FF + hi])
            t = _dot((_silu(gate) * up).astype(BF16), wout_ref[lo:hi, :])
            acc = t if acc is None else acc + t
        o_ref[rows, :] = x1 + g2_ref[...] * acc

    heads = [head(rows) for rows in halves]
    for rows, (x1, h2) in zip(halves, heads):
        ffn(rows, x1, h2)


def _na_mix_ffn_kernel(a_ref, *rest):
    def mix(wo_ref, rows):
        parts = [_dot(a_ref[g, rows, :], wo_ref[g * GROUP_W:(g + 1) * GROUP_W, :]) for g in range(a_ref.shape[0])]
        return functools.reduce(jnp.add, parts)

    _mix_ffn_tail(mix, *rest)


def _hg_mix_ffn_kernel(of_ref, ob_ref, sg_ref, gn_ref, *rest):
    def mix(wo_ref, rows):
        o = of_ref[rows, :].astype(F32) + ob_ref[rows, :].astype(F32)
        parts = []
        for h in range(HG_HEADS):
            oh = o[:, h * HG_DK:(h + 1) * HG_DK]
            ms = jnp.mean(oh * oh, axis=-1, keepdims=True)
            parts.append(oh * lax.rsqrt(ms + EPS))
        a = (jnp.concatenate(parts, axis=1) * gn_ref[...] * sg_ref[rows, :].astype(F32)).astype(BF16)
        return _dot(a, wo_ref[...])

    _mix_ffn_tail(mix, *rest)


def _mix_ffn(kernel, name, lead, lead_specs, x, g1, n2, sc2, sh2, g2, wo, win, wout, tm):
    B, T, _ = x.shape
    return pl.pallas_call(
        kernel,
        grid=(B, T // tm),
        in_specs=list(lead_specs) + [
            _tok_spec(tm), _mod_spec(g1), _resident((1, D_MODEL)), _mod_spec(sc2), _mod_spec(sh2), _mod_spec(g2),
            _resident(wo.shape), _resident(win.shape), _resident(wout.shape)],
        out_specs=_tok_spec(tm),
        out_shape=jax.ShapeDtypeStruct((B, T, D_MODEL), F32),
        compiler_params=_params("parallel", "parallel"),
        name=name,
    )(*lead, x, g1, n2, sc2, sh2, g2, wo, win, wout)


def _hg_proj_kernel(x_ref, g_ref, sc_ref, sh_ref, w_ref, lb_ref, tri3_ref, q_ref, v_ref, sg_ref, kk_ref, cum_ref):
    h = _norm_mod(x_ref[...], g_ref[...], sc_ref[...], sh_ref[...]).astype(BF16)

    def project(n):
        return _dot(h, w_ref[:, n * D_MODEL:(n + 1) * D_MODEL])

    def gates(d, f_pre):
        lb = lb_ref[d]
        half = 0.5 * (1.0 - lb)
        th = jnp.tanh(0.5 * f_pre)
        f = jnp.maximum(0.5 * (1.0 + lb) + half * th, GATE_FLOOR)
        kk_ref[d] = (half - half * th).astype(BF16)
        lf = jnp.log2(f)
        hi = lf.astype(BF16)
        mid, lo = _split_bf16(lf - hi.astype(F32))
        for c in range(f_pre.shape[0] // SCAN_CHUNK):
            rows = slice(c * SCAN_CHUNK, (c + 1) * SCAN_CHUNK)
            parts = jnp.concatenate([hi[rows], mid[rows], lo[rows]], axis=0)
            cum_ref[d, rows, :] = _dot(tri3_ref[d], parts)

    tails = [lambda z: q_ref.__setitem__(Ellipsis, _silu(z).astype(BF16)),
             lambda z: v_ref.__setitem__(Ellipsis, z.astype(BF16)),
             lambda z: sg_ref.__setitem__(Ellipsis, _silu(z).astype(BF16)),
             functools.partial(gates, 0), functools.partial(gates, 1)]
    pending = project(0)
    for n, tail in enumerate(tails):
        z = pending
        if n + 1 < len(tails):
            pending = project(n + 1)
        tail(z)


def _hg_proj(x, g, sc, sh, w, lb, tri3, tm):
    B, T, _ = x.shape
    tok = jax.ShapeDtypeStruct((B, T, D_MODEL), BF16)
    two = pl.BlockSpec((2, None, tm, D_MODEL), lambda b, i: (0, b, i, 0))
    return pl.pallas_call(
        _hg_proj_kernel,
        grid=(B, T // tm),
        in_specs=[_tok_spec(tm), _resident((1, D_MODEL)), _mod_spec(sc), _mod_spec(sh),
                  _resident(w.shape), _resident(lb.shape), _resident(tri3.shape)],
        out_specs=[_tok_spec(tm)] * 3 + [two, two],
        out_shape=[tok] * 3 + [jax.ShapeDtypeStruct((2, B, T, D_MODEL), BF16),
                               jax.ShapeDtypeStruct((2, B, T, D_MODEL), F32)],
        compiler_params=_params("parallel", "parallel"),
        name="hg_proj",
    )(x, g, sc, sh, w, lb, tri3)


def _hg_scan_kernel(qf_ref, vf_ref, kkf_ref, cumf_ref, qb_ref, vb_ref, kkb_ref, cumb_ref, tri_ref, s0_ref,
                    of_ref, ob_ref, st_ref, a_scr):
    @pl.when(pl.program_id(1) == 0)
    def _():
        st_ref[...] = s0_ref[...]

    C = SCAN_CHUNK
    n_sub = qf_ref.shape[0] // C
    dirs = ((qf_ref, vf_ref, kkf_ref, cumf_ref, of_ref), (qb_ref, vb_ref, kkb_ref, cumb_ref, ob_ref))
    streams = [(d, slice((j if d == 0 else n_sub - 1 - j) * C, (j if d == 0 else n_sub - 1 - j) * C + C))
               for j in range(n_sub) for d in range(2)]
    pair_w = 2 * HG_DK
    pairs = [slice(p * pair_w, (p + 1) * pair_w) for p in range(HG_HEADS // 2)]
    first = lax.broadcasted_iota(jnp.int32, (1, pair_w), 1) < HG_DK

    def block_diag(x):
        zero = jnp.zeros_like(x)
        return jnp.concatenate([jnp.where(first, x, zero), jnp.where(first, zero, x)], axis=0)

    def chunk_total(d, rows):
        last = rows.stop - 1 if d == 0 else rows.start
        return dirs[d][3][last:last + 1, :]

    out_of_range = [jnp.max(-0.5 * chunk_total(d, rows)) > FACTOR_RANGE for d, rows in streams]

    def stage_prep(n):
        d, rows = streams[n]
        q_ref, _, kk_ref, cum_ref, _ = dirs[d]
        b = cum_ref[rows, :]
        tot = chunk_total(d, rows)
        mid = 0.5 * tot
        q = q_ref[rows, :]
        kk = kk_ref[rows, :]
        return dict(q_in=q * jnp.exp2(b - mid).astype(BF16), k_in=kk * jnp.exp2(mid - b).astype(BF16),
                    q_st=q * jnp.exp2(b).astype(BF16), k_st=kk * jnp.exp2(tot - b).astype(BF16),
                    decay=jnp.exp2(tot))

    def stage_intra(n, p, check_range):
        d, rows = streams[n]
        seen = jnp.concatenate([tri_ref[d]] * 2, axis=1) > 0.0
        a_s = [jnp.where(seen, _dot_nt(p["q_in"][:, pr], block_diag(p["k_in"][:, pr])), 0.0).astype(BF16)
               for pr in pairs]
        if not check_range:
            return a_s
        a_scr[n] = jnp.concatenate(a_s, axis=1)

        @pl.when(out_of_range[n])
        def _():
            a_scr[n] = _pairwise_intra(dirs[d][0][rows, :].astype(F32), dirs[d][2][rows, :].astype(F32),
                                       dirs[d][3], rows, d == 0).astype(BF16)

        return [a_scr[n, :, i * 2 * C:(i + 1) * 2 * C] for i in range(len(pairs))]

    def stage_values(n, p, a_s):
        d, rows = streams[n]
        v_ref = dirs[d][1]
        o_intra = [_dot(a, block_diag(v_ref[rows, pr])) for a, pr in zip(a_s, pairs)]
        update = [_dot_tn(v_ref[rows, h * HG_DK:(h + 1) * HG_DK], p["k_st"][:, h * HG_DK:(h + 1) * HG_DK])
                  for h in range(HG_HEADS)]
        return o_intra, update

    def stage_state(n, p, o_intra, update):
        d, rows = streams[n]
        zero = jnp.zeros((HG_DK, HG_DK), BF16)
        outs = []
        for i, pr in enumerate(pairs):
            st_a, st_b = st_ref[d, 2 * i], st_ref[d, 2 * i + 1]
            st_bd = jnp.concatenate([jnp.concatenate([st_a.astype(BF16), zero], axis=1),
                                     jnp.concatenate([zero, st_b.astype(BF16)], axis=1)], axis=0)
            outs.append(o_intra[i] + _dot_nt(p["q_st"][:, pr], st_bd))
            st_ref[d, 2 * i] = st_a * p["decay"][:, pr][:, :HG_DK] + update[2 * i]
            st_ref[d, 2 * i + 1] = st_b * p["decay"][:, pr][:, HG_DK:] + update[2 * i + 1]
        dirs[d][4][rows, :] = jnp.concatenate(outs, axis=1).astype(BF16)

    def run(check_range):
        n_str = len(streams)
        prep, intra, vals = {}, {}, {}
        for t in range(n_str + 3):
            if t < n_str:
                prep[t] = stage_prep(t)
            if 0 <= t - 1 < n_str:
                intra[t - 1] = stage_intra(t - 1, prep[t - 1], check_range)
            if 0 <= t - 2 < n_str:
                vals[t - 2] = stage_values(t - 2, prep[t - 2], intra.pop(t - 2))
            if 0 <= t - 3 < n_str:
                stage_state(t - 3, prep.pop(t - 3), *vals.pop(t - 3))

    any_out = functools.reduce(jnp.logical_or, out_of_range)
    pl.when(jnp.logical_not(any_out))(lambda: run(False))
    pl.when(any_out)(lambda: run(True))


def _pairwise_intra(q, kk, cum_ref, rows, forward):
    C = q.shape[0]
    b = cum_ref[rows, :]
    t_idx = lax.broadcasted_iota(jnp.int32, (C, 1), 0)
    lane = lax.broadcasted_iota(jnp.int32, (1, HG_HEADS * C), 1)

    def body(s, acc):
        b_s = cum_ref[pl.ds(rows.start + s, 1), :]
        k_s = jnp.sum(jnp.where(t_idx == s, kk, 0.0), axis=0, keepdims=True)
        w = q * (k_s * jnp.exp2(jnp.minimum(b - b_s, 0.0)))
        seen = (t_idx >= s) if forward else (t_idx <= s)
        for h in range(HG_HEADS):
            col = jnp.sum(w[:, h * HG_DK:(h + 1) * HG_DK], axis=1, keepdims=True)
            acc = jnp.where(lane == h * C + s, jnp.where(seen, col, 0.0), acc)
        return acc

    return lax.fori_loop(0, C, body, jnp.zeros((C, HG_HEADS * C), F32))


def _hg_scan(q, v, kk, cum, tri, s0):
    B, T, _ = q.shape
    C = SCAN_CHUNK * max(s for s in range(1, SCAN_SUB + 1) if T % (SCAN_CHUNK * s) == 0)
    nc = T // C
    fwd = pl.BlockSpec((None, C, D_MODEL), lambda b, i: (b, i, 0))
    bwd = pl.BlockSpec((None, C, D_MODEL), lambda b, i: (b, nc - 1 - i, 0))
    dfwd = pl.BlockSpec((None, None, C, D_MODEL), lambda b, i: (0, b, i, 0))
    dbwd = pl.BlockSpec((None, None, C, D_MODEL), lambda b, i: (1, b, nc - 1 - i, 0))
    state = pl.BlockSpec((2, None, HG_HEADS, HG_DK, HG_DK), lambda b, i: (0, b, 0, 0, 0))
    tok = jax.ShapeDtypeStruct((B, T, D_MODEL), BF16)
    return pl.pallas_call(
        _hg_scan_kernel,
        grid=(B, nc),
        in_specs=[fwd, fwd, dfwd, dfwd, bwd, bwd, dbwd, dbwd, _resident(tri.shape), state],
        out_specs=[fwd, bwd, state],
        out_shape=[tok, tok, jax.ShapeDtypeStruct((2, B, HG_HEADS, HG_DK, HG_DK), F32)],
        scratch_shapes=[pltpu.VMEM((2 * C // SCAN_CHUNK, SCAN_CHUNK, HG_HEADS * SCAN_CHUNK), BF16)],
        compiler_params=_params("parallel", "arbitrary"),
        name="hg_scan",
    )(q, v, kk, cum, q, v, kk, cum, tri, s0)


def _tile(n, pref):
    return pref if n % pref == 0 else n


def kernel(x, c, ctx, c_ctx, ada_w, ada_b, norm1_g, norm2_g, na_w_qkv, na_w_o, na_q_gain, na_k_gain, na_rpb,
           hg_w_in, hg_lower, hg_norm_g, hg_w_o, ffn_w_in, ffn_w_out):
    B, T, _ = x.shape
    L = ctx.shape[1]
    rows = T // GRID_W
    assert B + 1 <= 8 and T % (Q_BLOCKS_PER_STEP * Q_BLK) == 0 and rows >= WIN_H
    assert T % SCAN_CHUNK == 0 and L % Q_BLK == 0
    tm = _tile(T, 512)
    tl = _tile(L, 256)
    row = lambda a: a.reshape(1, -1).astype(F32)

    cvec = jnp.zeros((8, D_MODEL), F32).at[:B].set(c).at[B].set(c_ctx)
    mods = _ada(cvec, ada_w, ada_b)

    def mod_rows(i):
        lat = [m.reshape(B, 1, D_MODEL) for m in jnp.split(mods[i, :B], N_MOD, axis=-1)]
        cx = [m.reshape(1, 1, D_MODEL) for m in jnp.split(mods[i, B], N_MOD, axis=-1)]
        return lat, cx

    (sh1, sc1, g1, sh2, sc2, g2), (csh1, csc1, cg1, csh2, csc2, cg2) = mod_rows(0)
    w_qk = na_w_qkv[0, :, :2 * D_MODEL].astype(BF16)
    w_vt = na_w_qkv[0, :, 2 * D_MODEL:].T.astype(BF16)
    gq = row(jnp.tile(na_q_gain[0], NA_HEADS) * (NA_HEAD_DIM ** -0.5 * LOG2E))
    gk = row(jnp.tile(na_k_gain[0], NA_HEADS))
    head_of = np.arange(D_MODEL) // NA_HEAD_DIM
    red = jnp.asarray(head_of[:, None] == np.arange(LANES)[None, :], BF16)
    expand = jnp.asarray(np.arange(2 * LANES)[:, None] % LANES == head_of[None, :], BF16)
    n1 = row(norm1_g[0])
    q_l, k_l, vt_l = _qkv(x, n1, sc1, sh1, w_qk, w_vt, gq, gk, red, expand, tm)
    q_c, k_c, vt_c = _qkv(ctx, n1, csc1, csh1, w_qk, w_vt, gq, gk, red, expand, tl)
    rpb2 = na_rpb[0].astype(F32) * LOG2E
    qk_max = NA_HEAD_DIM * jnp.max(jnp.abs(gq)) * jnp.max(jnp.abs(gk))
    bound = qk_max + jnp.maximum(jnp.max(rpb2), 0.0)
    shift = jnp.stack([-bound, (bound + qk_max < BOUNDED_SPREAD).astype(F32)])
    bias, row_mask = _attn_bias(rpb2, rows, -bound)
    a_l = _attn(shift, q_l, k_l, vt_l, k_c, vt_c, bias[None] + row_mask[:, None])
    a_c = _attn_ctx(q_c, k_c, vt_c)
    ffn = (na_w_o[0].astype(BF16), ffn_w_in[0].astype(BF16), ffn_w_out[0].astype(BF16))
    n2 = row(norm2_g[0])
    grp_spec = lambda t: pl.BlockSpec((D_MODEL // GROUP_W, None, t, GROUP_W), lambda b, i: (0, b, i, 0))
    x_lat = _mix_ffn(_na_mix_ffn_kernel, "na_mix_ffn", (a_l,), (grp_spec(tm),),
                     x, g1, n2, sc2, sh2, g2, *ffn, tm)
    x_ctx = _mix_ffn(_na_mix_ffn_kernel, "na_mix_ffn_ctx", (a_c,), (grp_spec(tl),),
                     ctx, cg1, n2, csc2, csh2, cg2, *ffn, tl)

    (sh1, sc1, g1, sh2, sc2, g2), (csh1, csc1, _, _, _, _) = mod_rows(1)
    lbs = jnp.cumsum(jax.nn.softmax(hg_lower.astype(F32), axis=0), axis=0)
    lb = (lbs - lbs[:1])[1].reshape(2, 1, D_MODEL)
    w_in = hg_w_in[0].astype(BF16)
    n1 = row(norm1_g[1])
    t_idx = np.arange(SCAN_CHUNK)
    tri_np = np.stack([t_idx[None, :] <= t_idx[:, None], t_idx[None, :] >= t_idx[:, None]])
    tri = jnp.asarray(tri_np, F32)
    tri3 = jnp.asarray(np.tile(tri_np, (1, 1, 3)), BF16)
    q_c, v_c, _, kk_c, cum_c = _hg_proj(x_ctx, n1, csc1, csh1, w_in, lb, tri3, tl)
    q_l, v_l, sg_l, kk_l, cum_l = _hg_proj(x_lat, n1, sc1, sh1, w_in, lb, tri3, tm)
    zero = jnp.zeros((2, B, HG_HEADS, HG_DK, HG_DK), F32)
    _, _, s_ctx = _hg_scan(q_c, v_c, kk_c, cum_c, tri, zero)
    o_f, o_b, _ = _hg_scan(q_l, v_l, kk_l, cum_l, tri, s_ctx)
    ffn = (hg_w_o[0].astype(BF16), ffn_w_in[1].astype(BF16), ffn_w_out[1].astype(BF16))
    gn = row(jnp.tile(hg_norm_g[0], HG_HEADS))
    return _mix_ffn(_hg_mix_ffn_kernel, "hg_mix_ffn", (o_f, o_b, sg_l, gn),
                    (_tok_spec(tm), _tok_spec(tm), _tok_spec(tm), _resident((1, D_MODEL))),
                    x_lat, g1, row(norm2_g[1]), sc2, sh2, g2, *ffn, tm)
```

```python
import functools

import jax
import jax.numpy as jnp
import numpy as np
from jax import lax
from jax.experimental import pallas as pl
from jax.experimental.pallas import tpu as pltpu

F32 = jnp.float32
BF16 = jnp.bfloat16

D_MODEL = 1024
EPS = 1e-6
GRID_W = 64
WIN_H = 8
WIN_W = 16
NA_HEADS = 16
NA_HEAD_DIM = D_MODEL // NA_HEADS
HG_HEADS = 8
HG_DK = 128
D_FF = 2816
N_MOD = 6

LANES = 128
Q_ROWS = 4
Q_BLK = Q_ROWS * GRID_W
KEY_SLOTS = 3
ATTN_GROUP = 4
GROUP_W = ATTN_GROUP * NA_HEAD_DIM
BOUNDED_SPREAD = 100.0
Q_BLOCKS_PER_STEP = 8
LOG2E = 1.4426950408889634
MASKED = -1e30
SCAN_CHUNK = 64
SCAN_SUB = 8
FACTOR_RANGE = 100.0
GATE_FLOOR = 2.0 ** -100
VMEM_LIMIT = 56 * 1024 * 1024


def _dot(a, b):
    return jnp.dot(a, b, preferred_element_type=F32)


def _dot_nt(a, b):
    return lax.dot_general(a, b, (((1,), (1,)), ((), ())), preferred_element_type=F32)


def _dot_tn(a, b):
    return lax.dot_general(a, b, (((0,), (0,)), ((), ())), preferred_element_type=F32)


def _silu(x):
    half = 0.5 * x
    return half + half * jnp.tanh(half)


def _norm_mod(xf, g, sc, sh):
    ms = jnp.mean(xf * xf, axis=-1, keepdims=True)
    return (xf * lax.rsqrt(ms + EPS) * g) * (1.0 + sc) + sh


def _split_bf16(x):
    hi = x.astype(BF16)
    lo = (x - hi.astype(F32)).astype(BF16)
    return hi, lo


def _resident(shape):
    zeros = (0,) * len(shape)
    return pl.BlockSpec(shape, lambda *_: zeros, pipeline_mode=pl.Buffered(1))


def _mod_spec(arr):
    if arr.shape[0] == 1:
        return pl.BlockSpec((None, 1, D_MODEL), lambda b, i: (0, 0, 0))
    return pl.BlockSpec((None, 1, D_MODEL), lambda b, i: (b, 0, 0))


def _tok_spec(tm, width=D_MODEL):
    return pl.BlockSpec((None, tm, width), lambda b, i: (b, i, 0))


def _params(*sem):
    return pltpu.CompilerParams(dimension_semantics=sem, vmem_limit_bytes=VMEM_LIMIT)


def _ada_kernel(c_ref, w_ref, b_ref, o_ref):
    s = _silu(c_ref[...]).astype(BF16)
    o_ref[...] = _dot(s, w_ref[...].astype(BF16)) + b_ref[...]


def _ada(cvec, ada_w, ada_b):
    depth, _, n = ada_w.shape
    tn = 1536
    return pl.pallas_call(
        _ada_kernel,
        grid=(depth, n // tn),
        in_specs=[pl.BlockSpec((8, D_MODEL), lambda l, j: (0, 0)),
                  pl.BlockSpec((None, D_MODEL, tn), lambda l, j: (l, 0, j)),
                  pl.BlockSpec((None, 1, tn), lambda l, j: (l, 0, j))],
        out_specs=pl.BlockSpec((None, 8, tn), lambda l, j: (l, 0, j)),
        out_shape=jax.ShapeDtypeStruct((depth, 8, n), F32),
        compiler_params=_params("arbitrary", "arbitrary"),
        name="ada",
    )(cvec, ada_w, ada_b.reshape(depth, 1, n))


def _qkv_kernel(x_ref, g_ref, sc_ref, sh_ref, w_ref, wvt_ref, gq_ref, gk_ref, red_ref, exp_ref,
                q_ref, k_ref, vt_ref):
    h = _norm_mod(x_ref[...], g_ref[...], sc_ref[...], sh_ref[...]).astype(BF16)

    def head_norm(z, gain, out_ref):
        ss = _dot((z * z).astype(BF16), red_ref[...])
        r = lax.rsqrt(ss * (1.0 / NA_HEAD_DIM) + EPS)
        rex = _dot(jnp.concatenate(_split_bf16(r), axis=1), exp_ref[...])
        zn = (z * rex * gain).astype(BF16)
        for g in range(D_MODEL // GROUP_W):
            out_ref[g] = zn[:, g * GROUP_W:(g + 1) * GROUP_W]

    q = _dot(h, w_ref[:, 0:D_MODEL])
    k = _dot(h, w_ref[:, D_MODEL:2 * D_MODEL])
    head_norm(q, gq_ref[...], q_ref)
    vt = _dot_nt(wvt_ref[...], h).astype(BF16)
    head_norm(k, gk_ref[...], k_ref)
    for j in range(vt_ref.shape[0]):
        vt_ref[j] = vt[:, j * Q_BLK:(j + 1) * Q_BLK]


def _qkv(x, g, sc, sh, w_qk, w_vt, gq, gk, red, expand, tm):
    B, T, _ = x.shape
    n_grp = D_MODEL // GROUP_W
    grp = jax.ShapeDtypeStruct((n_grp, B, T, GROUP_W), BF16)
    grp_spec = pl.BlockSpec((n_grp, None, tm, GROUP_W), lambda b, i: (0, b, i, 0))
    return pl.pallas_call(
        _qkv_kernel,
        grid=(B, T // tm),
        in_specs=[_tok_spec(tm), _resident((1, D_MODEL)), _mod_spec(sc), _mod_spec(sh),
                  _resident(w_qk.shape), _resident(w_vt.shape), _resident((1, D_MODEL)), _resident((1, D_MODEL)),
                  _resident(red.shape), _resident(expand.shape)],
        out_specs=[grp_spec, grp_spec,
                   pl.BlockSpec((None, tm // Q_BLK, D_MODEL, Q_BLK), lambda b, i: (b, i, 0, 0))],
        out_shape=[grp, grp, jax.ShapeDtypeStruct((B, T // Q_BLK, D_MODEL, Q_BLK), BF16)],
        compiler_params=_params("parallel", "parallel"),
        name="qkv",
    )(x, g, sc, sh, w_qk, w_vt, gq, gk, red, expand)


def _attend(tasks):
    def stage_scores(task):
        q_h, keys, _, biases = task
        s = [_dot_nt(key(), q_h) + bias() for key, bias in zip(keys, biases)]
        m = functools.reduce(jnp.maximum, [jnp.max(t, axis=0, keepdims=True) for t in s])
        return s, m

    def stage_probs(sm):
        s, m = sm
        return [jnp.exp2(t - m).astype(BF16) for t in s]

    n = len(tasks)
    scores, probs, outs = {}, {}, []
    for step in range(n + 2):
        if step < n:
            scores[step] = stage_scores(tasks[step])
        if 0 <= step - 1 < n:
            probs[step - 1] = stage_probs(scores.pop(step - 1))
        if 0 <= step - 2 < n:
            outs.append(_weighted_values(tasks[step - 2], probs.pop(step - 2)))
    return outs


def _attend_bounded(tasks):
    def stage_probs(task):
        q_h, keys, _, biases = task
        return [jnp.exp2(_dot_nt(key(), q_h) + bias()).astype(BF16) for key, bias in zip(keys, biases)]

    n = len(tasks)
    probs, outs = {}, []
    for step in range(n + 1):
        if step < n:
            probs[step] = stage_probs(tasks[step])
        if step >= 1:
            outs.append(_weighted_values(tasks[step - 1], probs.pop(step - 1)))
    return outs


def _weighted_values(task, p):
    ones_rows = 16
    acc = None
    for vt, t in zip(task[2], p):
        v1 = jnp.concatenate([vt(), jnp.ones((ones_rows, t.shape[0]), BF16)], axis=0)
        part = _dot(v1, t)
        acc = part if acc is None else acc + part
    return acc[:NA_HEAD_DIM] * (1.0 / acc[NA_HEAD_DIM:NA_HEAD_DIM + 1])


def _head_queries(q_grp):
    lane_head = lax.broadcasted_iota(jnp.int32, (1, q_grp.shape[1]), 1) // NA_HEAD_DIM
    return [jnp.where(lane_head == h, q_grp, jnp.zeros_like(q_grp)) for h in range(ATTN_GROUP)]


def _head_rows(ref, h):
    return lambda: ref[h * NA_HEAD_DIM:(h + 1) * NA_HEAD_DIM, :]


def _attn_kernel(shift_ref, q_ref, *refs):
    n_halo = Q_BLOCKS_PER_STEP + KEY_SLOTS - 1
    k_refs, v_refs = refs[:n_halo], refs[n_halo:2 * n_halo]
    kc_ref, vc_ref = refs[2 * n_halo:2 * n_halo + 2]
    bias_first, bias_mid, bias_last, o_ref = refs[2 * n_halo + 2:]
    bias_refs = [bias_first] + [bias_mid] * (Q_BLOCKS_PER_STEP - 2) + [bias_last]
    n_ctx = kc_ref.shape[0] // Q_BLK
    shift = shift_ref[0]
    tasks = []
    for jb in range(Q_BLOCKS_PER_STEP):
        q_heads = _head_queries(q_ref[jb * Q_BLK:(jb + 1) * Q_BLK, :])
        for h in range(ATTN_GROUP):
            keys = ([(lambda r=r: r[...]) for r in k_refs[jb:jb + KEY_SLOTS]]
                    + [(lambda t=t: kc_ref[t * Q_BLK:(t + 1) * Q_BLK, :]) for t in range(n_ctx)])
            vts = ([_head_rows(r, h) for r in v_refs[jb:jb + KEY_SLOTS]]
                   + [_head_rows(vc_ref.at[t], h) for t in range(n_ctx)])
            biases = ([(lambda t=t, jb=jb, h=h: bias_refs[jb][h, t * Q_BLK:(t + 1) * Q_BLK, :])
                       for t in range(KEY_SLOTS)] + [lambda: shift] * n_ctx)
            tasks.append((q_heads[h], keys, vts, biases))

    def finish(outs):
        for jb in range(Q_BLOCKS_PER_STEP):
            o_t = jnp.concatenate(outs[jb * ATTN_GROUP:(jb + 1) * ATTN_GROUP], axis=0)
            o_ref[jb * Q_BLK:(jb + 1) * Q_BLK, :] = o_t.T.astype(BF16)

    bounded = shift_ref[1] > 0.5

    @pl.when(bounded)
    def _():
        finish(_attend_bounded(tasks))

    @pl.when(jnp.logical_not(bounded))
    def _():
        finish(_attend(tasks))


def _attn(shift, q, k, vt, kc, vtc, bias):
    n_grp, B, T, gw = q.shape
    L = kc.shape[2]
    nb = T // Q_BLK
    per = Q_BLOCKS_PER_STEP
    n_halo = per + KEY_SLOTS - 1

    def blk(i, d):
        return jnp.clip(per * i + d, 0, nb - 1)

    def variant(qb):
        return jnp.where(qb == 0, 0, jnp.where(qb == nb - 1, 2, 1))

    halo = [pl.BlockSpec((None, None, Q_BLK, gw), lambda g, b, i, d=d: (g, b, blk(i, d - 1), 0))
            for d in range(n_halo)]
    halo_t = [pl.BlockSpec((None, None, gw, Q_BLK), lambda g, b, i, d=d: (b, blk(i, d - 1), g, 0))
              for d in range(n_halo)]
    biases = [pl.BlockSpec((None, ATTN_GROUP, KEY_SLOTS * Q_BLK, Q_BLK), index)
              for index in (lambda g, b, i: (variant(per * i), g, 0, 0),
                            lambda g, b, i: (1, g, 0, 0),
                            lambda g, b, i: (variant(per * i + per - 1), g, 0, 0))]
    tok = pl.BlockSpec((None, None, per * Q_BLK, gw), lambda g, b, i: (g, b, i, 0))
    return pl.pallas_call(
        _attn_kernel,
        grid=(n_grp, B, nb // per),
        in_specs=[pl.BlockSpec(memory_space=pltpu.SMEM), tok] + halo + halo_t
                 + [pl.BlockSpec((None, None, L, gw), lambda g, b, i: (g, b, 0, 0)),
                    pl.BlockSpec((None, L // Q_BLK, gw, Q_BLK), lambda g, b, i: (b, 0, g, 0))] + biases,
        out_specs=tok,
        out_shape=jax.ShapeDtypeStruct((n_grp, B, T, gw), BF16),
        compiler_params=_params("parallel", "parallel", "parallel"),
        name="attn",
    )(shift, q, *([k] * n_halo), *([vt] * n_halo), kc, vtc, bias, bias, bias)


def _attn_ctx_kernel(q_ref, k_ref, vt_ref, o_ref):
    q_heads = _head_queries(q_ref[...])
    n_ctx = k_ref.shape[0] // Q_BLK
    tasks = [(q_heads[h],
              [(lambda t=t: k_ref[t * Q_BLK:(t + 1) * Q_BLK, :]) for t in range(n_ctx)],
              [_head_rows(vt_ref.at[t], h) for t in range(n_ctx)],
              [lambda: 0.0] * n_ctx) for h in range(ATTN_GROUP)]
    o_ref[...] = jnp.concatenate(_attend(tasks), axis=0).T.astype(BF16)


def _attn_ctx(q, k, vt):
    n_grp, B, L, gw = q.shape
    spec = pl.BlockSpec((None, None, L, gw), lambda b, g: (g, b, 0, 0))
    return pl.pallas_call(
        _attn_ctx_kernel,
        grid=(B, n_grp),
        in_specs=[spec, spec, pl.BlockSpec((None, L // Q_BLK, gw, Q_BLK), lambda b, g: (b, 0, g, 0))],
        out_specs=spec,
        out_shape=jax.ShapeDtypeStruct((n_grp, B, L, gw), BF16),
        compiler_params=_params("parallel", "parallel"),
        name="attn_ctx",
    )(q, k, vt)


def _attn_bias(rpb, rows, shift):
    n_heads = rpb.shape[0]
    n_slot = KEY_SLOTS * Q_ROWS
    cc = np.arange(GRID_W)[:, None]
    c = np.arange(GRID_W)[None, :]
    onehot = (cc - c + WIN_W - 1)[None] == np.arange(2 * WIN_W - 1)[:, None, None]
    toep = jnp.einsum('hdk,kxc->hdxc', rpb, jnp.asarray(onehot, F32), precision=lax.Precision.HIGHEST)
    col_start = np.clip(c - WIN_W // 2, 0, GRID_W - WIN_W)
    toep = jnp.where((cc >= col_start) & (cc < col_start + WIN_W), toep + shift, MASKED)
    lo_d = WIN_H - 1 - Q_ROWS
    bias = jnp.concatenate([toep[:, lo_d - a:lo_d - a + n_slot] for a in range(Q_ROWS)], axis=-1)
    full = (n_slot, GRID_W, Q_ROWS, GRID_W)
    j = np.arange(n_slot)[:, None, None, None]
    a = np.arange(Q_ROWS)[None, None, :, None]
    kh = min(WIN_H, rows)
    lows = (np.full_like(a, Q_ROWS), a + Q_ROWS - kh // 2, np.full_like(a, 2 * Q_ROWS - kh))
    row_mask = np.stack([np.where(np.broadcast_to((j >= lo) & (j < lo + kh), full), 0.0, MASKED) for lo in lows])
    return (bias.reshape(n_heads, KEY_SLOTS * Q_BLK, Q_BLK),
            jnp.asarray(row_mask.reshape(3, KEY_SLOTS * Q_BLK, Q_BLK), F32))


MXU_DEPTH = 256
FF_BOUNDS = (0, MXU_DEPTH * (D_FF // MXU_DEPTH + 1) // 2, D_FF)


def _mix_ffn_tail(mix, x_ref, g1_ref, n2_ref, sc2_ref, sh2_ref, g2_ref, wo_ref, win_ref, wout_ref, o_ref):
    tm = x_ref.shape[0]
    n_sub = 2 if tm % (2 * Q_BLK) == 0 else 1
    halves = [slice(i * (tm // n_sub), (i + 1) * (tm // n_sub)) for i in range(n_sub)]

    def head(rows):
        x1 = x_ref[rows, :] + g1_ref[...] * mix(wo_ref, rows)
        return x1, _norm_mod(x1, n2_ref[...], sc2_ref[...], sh2_ref[...]).astype(BF16)

    def ffn(rows, x1, h2):
        acc = None
        for lo, hi in zip(FF_BOUNDS[:-1], FF_BOUNDS[1:]):
            gate = _dot(h2, win_ref[:, lo:hi])
            up = _dot(h2, win_ref[:, D_FF + lo:D_FF + hi])
            t = _dot((_silu(gate) * up).astype(BF16), wout_ref[lo:hi, :])
            acc = t if acc is None else acc + t
        o_ref[rows, :] = x1 + g2_ref[...] * acc

    heads = [head(rows) for rows in halves]
    for rows, (x1, h2) in zip(halves, heads):
        ffn(rows, x1, h2)


def _na_mix_ffn_kernel(a_ref, *rest):
    def mix(wo_ref, rows):
        parts = [_dot(a_ref[g, rows, :], wo_ref[g * GROUP_W:(g + 1) * GROUP_W, :]) for g in range(a_ref.shape[0])]
        return functools.reduce(jnp.add, parts)

    _mix_ffn_tail(mix, *rest)


def _hg_mix_ffn_kernel(of_ref, ob_ref, sg_ref, gn_ref, *rest):
    def mix(wo_ref, rows):
        o = of_ref[rows, :].astype(F32) + ob_ref[rows, :].astype(F32)
        parts = []
        for h in range(HG_HEADS):
            oh = o[:, h * HG_DK:(h + 1) * HG_DK]
            ms = jnp.mean(oh * oh, axis=-1, keepdims=True)
            parts.append(oh * lax.rsqrt(ms + EPS))
        a = (jnp.concatenate(parts, axis=1) * gn_ref[...] * sg_ref[rows, :].astype(F32)).astype(BF16)
        return _dot(a, wo_ref[...])

    _mix_ffn_tail(mix, *rest)


def _mix_ffn(kernel, name, lead, lead_specs, x, g1, n2, sc2, sh2, g2, wo, win, wout, tm):
    B, T, _ = x.shape
    return pl.pallas_call(
        kernel,
        grid=(B, T // tm),
        in_specs=list(lead_specs) + [
            _tok_spec(tm), _mod_spec(g1), _resident((1, D_MODEL)), _mod_spec(sc2), _mod_spec(sh2), _mod_spec(g2),
            _resident(wo.shape), _resident(win.shape), _resident(wout.shape)],
        out_specs=_tok_spec(tm),
        out_shape=jax.ShapeDtypeStruct((B, T, D_MODEL), F32),
        compiler_params=_params("parallel", "parallel"),
        name=name,
    )(*lead, x, g1, n2, sc2, sh2, g2, wo, win, wout)


def _hg_proj_kernel(x_ref, g_ref, sc_ref, sh_ref, w_ref, lb_ref, tri3_ref, q_ref, v_ref, sg_ref, kk_ref, cum_ref):
    halves = [slice(0, x_ref.shape[0])]
    hs = [_norm_mod(x_ref[rows, :], g_ref[...], sc_ref[...], sh_ref[...]).astype(BF16) for rows in halves]

    def gates(d, rows, f_pre):
        lb = lb_ref[d]
        half = 0.5 * (1.0 - lb)
        th = jnp.tanh(0.5 * f_pre)
        f = jnp.maximum(0.5 * (1.0 + lb) + half * th, GATE_FLOOR)
        kk_ref[d, rows, :] = (half - half * th).astype(BF16)
        lf = jnp.log2(f)
        hi = lf.astype(BF16)
        mid, lo = _split_bf16(lf - hi.astype(F32))
        for c in range(f_pre.shape[0] // SCAN_CHUNK):
            sub = slice(c * SCAN_CHUNK, (c + 1) * SCAN_CHUNK)
            parts = jnp.concatenate([hi[sub], mid[sub], lo[sub]], axis=0)
            cum_ref[d, rows.start + sub.start:rows.start + sub.stop, :] = _dot(tri3_ref[d], parts)

    def store(ref, fn, rows, z):
        ref[rows, :] = fn(z).astype(BF16)

    work = []
    for rows, h in zip(halves, hs):
        work += [(h, 3, functools.partial(gates, 0, rows)),
                 (h, 4, functools.partial(gates, 1, rows)),
                 (h, 0, functools.partial(store, q_ref, _silu, rows)),
                 (h, 2, functools.partial(store, sg_ref, _silu, rows)),
                 (h, 1, functools.partial(store, v_ref, lambda z: z, rows))]

    def project(item):
        h, n, _ = item
        return _dot(h, w_ref[:, n * D_MODEL:(n + 1) * D_MODEL])

    pending = project(work[0])
    for n, item in enumerate(work):
        z = pending
        if n + 1 < len(work):
            pending = project(work[n + 1])
        item[2](z)


def _hg_proj(x, g, sc, sh, w, lb, tri3, tm):
    B, T, _ = x.shape
    tok = jax.ShapeDtypeStruct((B, T, D_MODEL), BF16)
    two = pl.BlockSpec((2, None, tm, D_MODEL), lambda b, i: (0, b, i, 0))
    return pl.pallas_call(
        _hg_proj_kernel,
        grid=(B, T // tm),
        in_specs=[_tok_spec(tm), _resident((1, D_MODEL)), _mod_spec(sc), _mod_spec(sh),
                  _resident(w.shape), _resident(lb.shape), _resident(tri3.shape)],
        out_specs=[_tok_spec(tm)] * 3 + [two, two],
        out_shape=[tok] * 3 + [jax.ShapeDtypeStruct((2, B, T, D_MODEL), BF16),
                               jax.ShapeDtypeStruct((2, B, T, D_MODEL), F32)],
        compiler_params=_params("parallel", "parallel"),
        name="hg_proj",
    )(x, g, sc, sh, w, lb, tri3)


def _hg_scan_kernel(qf_ref, vf_ref, kkf_ref, cumf_ref, qb_ref, vb_ref, kkb_ref, cumb_ref, tri_ref, s0_ref,
                    of_ref, ob_ref, st_ref, a_scr):
    @pl.when(pl.program_id(1) == 0)
    def _():
        st_ref[...] = s0_ref[...]

    C = SCAN_CHUNK
    n_sub = qf_ref.shape[0] // C
    dirs = ((qf_ref, vf_ref, kkf_ref, cumf_ref, of_ref), (qb_ref, vb_ref, kkb_ref, cumb_ref, ob_ref))
    streams = [(d, slice((j if d == 0 else n_sub - 1 - j) * C, (j if d == 0 else n_sub - 1 - j) * C + C))
               for j in range(n_sub) for d in range(2)]
    pair_w = 2 * HG_DK
    pairs = [slice(p * pair_w, (p + 1) * pair_w) for p in range(HG_HEADS // 2)]
    first = lax.broadcasted_iota(jnp.int32, (1, pair_w), 1) < HG_DK

    def block_diag(x):
        zero = jnp.zeros_like(x)
        return jnp.concatenate([jnp.where(first, x, zero), jnp.where(first, zero, x)], axis=0)

    def chunk_total(d, rows):
        last = rows.stop - 1 if d == 0 else rows.start
        return dirs[d][3][last:last + 1, :]

    out_of_range = [jnp.max(-0.5 * chunk_total(d, rows)) > FACTOR_RANGE for d, rows in streams]

    def stage_prep(n):
        d, rows = streams[n]
        q_ref, _, kk_ref, cum_ref, _ = dirs[d]
        b = cum_ref[rows, :]
        tot = chunk_total(d, rows)
        mid = 0.5 * tot
        q = q_ref[rows, :]
        kk = kk_ref[rows, :]
        return dict(q_in=q * jnp.exp2(b - mid).astype(BF16), k_in=kk * jnp.exp2(mid - b).astype(BF16),
                    q_st=q * jnp.exp2(b).astype(BF16), k_st=kk * jnp.exp2(tot - b).astype(BF16),
                    decay=jnp.exp2(tot))

    def stage_intra(n, p, check_range):
        d, rows = streams[n]
        seen = jnp.concatenate([tri_ref[d]] * 2, axis=1) > 0.0
        a_s = [jnp.where(seen, _dot_nt(p["q_in"][:, pr], block_diag(p["k_in"][:, pr])), 0.0).astype(BF16)
               for pr in pairs]
        if not check_range:
            return a_s
        a_scr[n] = jnp.concatenate(a_s, axis=1)

        @pl.when(out_of_range[n])
        def _():
            a_scr[n] = _pairwise_intra(dirs[d][0][rows, :].astype(F32), dirs[d][2][rows, :].astype(F32),
                                       dirs[d][3], rows, d == 0).astype(BF16)

        return [a_scr[n, :, i * 2 * C:(i + 1) * 2 * C] for i in range(len(pairs))]

    def stage_values(n, p, a_s):
        d, rows = streams[n]
        v_ref = dirs[d][1]
        o_intra = [_dot(a, block_diag(v_ref[rows, pr])) for a, pr in zip(a_s, pairs)]
        update = [_dot_tn(v_ref[rows, h * HG_DK:(h + 1) * HG_DK], p["k_st"][:, h * HG_DK:(h + 1) * HG_DK])
                  for h in range(HG_HEADS)]
        return o_intra, update

    def stage_state(n, p, o_intra, update):
        d, rows = streams[n]
        zero = jnp.zeros((HG_DK, HG_DK), BF16)
        outs = []
        for i, pr in enumerate(pairs):
            st_a, st_b = st_ref[d, 2 * i], st_ref[d, 2 * i + 1]
            st_bd = jnp.concatenate([jnp.concatenate([st_a.astype(BF16), zero], axis=1),
                                     jnp.concatenate([zero, st_b.astype(BF16)], axis=1)], axis=0)
            outs.append(o_intra[i] + _dot_nt(p["q_st"][:, pr], st_bd))
            st_ref[d, 2 * i] = st_a * p["decay"][:, pr][:, :HG_DK] + update[2 * i]
            st_ref[d, 2 * i + 1] = st_b * p["decay"][:, pr][:, HG_DK:] + update[2 * i + 1]
        dirs[d][4][rows, :] = jnp.concatenate(outs, axis=1).astype(BF16)

    def run(check_range):
        n_str = len(streams)
        prep, intra, vals = {}, {}, {}
        for t in range(n_str + 3):
            if t < n_str:
                prep[t] = stage_prep(t)
            if 0 <= t - 1 < n_str:
                intra[t - 1] = stage_intra(t - 1, prep[t - 1], check_range)
            if 0 <= t - 2 < n_str:
                vals[t - 2] = stage_values(t - 2, prep[t - 2], intra.pop(t - 2))
            if 0 <= t - 3 < n_str:
                stage_state(t - 3, prep.pop(t - 3), *vals.pop(t - 3))

    any_out = functools.reduce(jnp.logical_or, out_of_range)
    pl.when(jnp.logical_not(any_out))(lambda: run(False))
    pl.when(any_out)(lambda: run(True))


def _pairwise_intra(q, kk, cum_ref, rows, forward):
    C = q.shape[0]
    b = cum_ref[rows, :]
    t_idx = lax.broadcasted_iota(jnp.int32, (C, 1), 0)
    lane = lax.broadcasted_iota(jnp.int32, (1, HG_HEADS * C), 1)

    def body(s, acc):
        b_s = cum_ref[pl.ds(rows.start + s, 1), :]
        k_s = jnp.sum(jnp.where(t_idx == s, kk, 0.0), axis=0, keepdims=True)
        w = q * (k_s * jnp.exp2(jnp.minimum(b - b_s, 0.0)))
        seen = (t_idx >= s) if forward else (t_idx <= s)
        for h in range(HG_HEADS):
            col = jnp.sum(w[:, h * HG_DK:(h + 1) * HG_DK], axis=1, keepdims=True)
            acc = jnp.where(lane == h * C + s, jnp.where(seen, col, 0.0), acc)
        return acc

    return lax.fori_loop(0, C, body, jnp.zeros((C, HG_HEADS * C), F32))


def _hg_scan(q, v, kk, cum, tri, s0):
    B, T, _ = q.shape
    C = SCAN_CHUNK * max(s for s in range(1, SCAN_SUB + 1) if T % (SCAN_CHUNK * s) == 0)
    nc = T // C
    fwd = pl.BlockSpec((None, C, D_MODEL), lambda b, i: (b, i, 0))
    bwd = pl.BlockSpec((None, C, D_MODEL), lambda b, i: (b, nc - 1 - i, 0))
    dfwd = pl.BlockSpec((None, None, C, D_MODEL), lambda b, i: (0, b, i, 0))
    dbwd = pl.BlockSpec((None, None, C, D_MODEL), lambda b, i: (1, b, nc - 1 - i, 0))
    state = pl.BlockSpec((2, None, HG_HEADS, HG_DK, HG_DK), lambda b, i: (0, b, 0, 0, 0))
    tok = jax.ShapeDtypeStruct((B, T, D_MODEL), BF16)
    return pl.pallas_call(
        _hg_scan_kernel,
        grid=(B, nc),
        in_specs=[fwd, fwd, dfwd, dfwd, bwd, bwd, dbwd, dbwd, _resident(tri.shape), state],
        out_specs=[fwd, bwd, state],
        out_shape=[tok, tok, jax.ShapeDtypeStruct((2, B, HG_HEADS, HG_DK, HG_DK), F32)],
        scratch_shapes=[pltpu.VMEM((2 * C // SCAN_CHUNK, SCAN_CHUNK, HG_HEADS * SCAN_CHUNK), BF16)],
        compiler_params=_params("parallel", "arbitrary"),
        name="hg_scan",
    )(q, v, kk, cum, q, v, kk, cum, tri, s0)


def _tile(n, pref):
    return pref if n % pref == 0 else n


def kernel(x, c, ctx, c_ctx, ada_w, ada_b, norm1_g, norm2_g, na_w_qkv, na_w_o, na_q_gain, na_k_gain, na_rpb,
           hg_w_in, hg_lower, hg_norm_g, hg_w_o, ffn_w_in, ffn_w_out):
    B, T, _ = x.shape
    L = ctx.shape[1]
    rows = T // GRID_W
    assert B + 1 <= 8 and T % (Q_BLOCKS_PER_STEP * Q_BLK) == 0 and rows >= WIN_H
    assert T % SCAN_CHUNK == 0 and L % Q_BLK == 0
    tm = _tile(T, 512)
    tl = _tile(L, 256)
    row = lambda a: a.reshape(1, -1).astype(F32)

    cvec = jnp.zeros((8, D_MODEL), F32).at[:B].set(c).at[B].set(c_ctx)
    mods = _ada(cvec, ada_w, ada_b)

    def mod_rows(i):
        lat = [m.reshape(B, 1, D_MODEL) for m in jnp.split(mods[i, :B], N_MOD, axis=-1)]
        cx = [m.reshape(1, 1, D_MODEL) for m in jnp.split(mods[i, B], N_MOD, axis=-1)]
        return lat, cx

    (sh1, sc1, g1, sh2, sc2, g2), (csh1, csc1, cg1, csh2, csc2, cg2) = mod_rows(0)
    w_qk = na_w_qkv[0, :, :2 * D_MODEL].astype(BF16)
    w_vt = na_w_qkv[0, :, 2 * D_MODEL:].T.astype(BF16)
    gq = row(jnp.tile(na_q_gain[0], NA_HEADS) * (NA_HEAD_DIM ** -0.5 * LOG2E))
    gk = row(jnp.tile(na_k_gain[0], NA_HEADS))
    head_of = np.arange(D_MODEL) // NA_HEAD_DIM
    red = jnp.asarray(head_of[:, None] == np.arange(LANES)[None, :], BF16)
    expand = jnp.asarray(np.arange(2 * LANES)[:, None] % LANES == head_of[None, :], BF16)
    n1 = row(norm1_g[0])
    q_l, k_l, vt_l = _qkv(x, n1, sc1, sh1, w_qk, w_vt, gq, gk, red, expand, tm)
    q_c, k_c, vt_c = _qkv(ctx, n1, csc1, csh1, w_qk, w_vt, gq, gk, red, expand, tl)
    rpb2 = na_rpb[0].astype(F32) * LOG2E
    qk_max = NA_HEAD_DIM * jnp.max(jnp.abs(gq)) * jnp.max(jnp.abs(gk))
    bound = qk_max + jnp.maximum(jnp.max(rpb2), 0.0)
    shift = jnp.stack([-bound, (bound + qk_max < BOUNDED_SPREAD).astype(F32)])
    bias, row_mask = _attn_bias(rpb2, rows, -bound)
    a_l = _attn(shift, q_l, k_l, vt_l, k_c, vt_c, bias[None] + row_mask[:, None])
    a_c = _attn_ctx(q_c, k_c, vt_c)
    ffn = (na_w_o[0].astype(BF16), ffn_w_in[0].astype(BF16), ffn_w_out[0].astype(BF16))
    n2 = row(norm2_g[0])
    grp_spec = lambda t: pl.BlockSpec((D_MODEL // GROUP_W, None, t, GROUP_W), lambda b, i: (0, b, i, 0))
    x_lat = _mix_ffn(_na_mix_ffn_kernel, "na_mix_ffn", (a_l,), (grp_spec(tm),),
                     x, g1, n2, sc2, sh2, g2, *ffn, tm)
    x_ctx = _mix_ffn(_na_mix_ffn_kernel, "na_mix_ffn_ctx", (a_c,), (grp_spec(tl),),
                     ctx, cg1, n2, csc2, csh2, cg2, *ffn, tl)

    (sh1, sc1, g1, sh2, sc2, g2), (csh1, csc1, _, _, _, _) = mod_rows(1)
    lbs = jnp.cumsum(jax.nn.softmax(hg_lower.astype(F32), axis=0), axis=0)
    lb = (lbs - lbs[:1])[1].reshape(2, 1, D_MODEL)
    w_in = hg_w_in[0].astype(BF16)
    n1 = row(norm1_g[1])
    t_idx = np.arange(SCAN_CHUNK)
    tri_np = np.stack([t_idx[None, :] <= t_idx[:, None], t_idx[None, :] >= t_idx[:, None]])
    tri = jnp.asarray(tri_np, F32)
    tri3 = jnp.asarray(np.tile(tri_np, (1, 1, 3)), BF16)
    q_c, v_c, _, kk_c, cum_c = _hg_proj(x_ctx, n1, csc1, csh1, w_in, lb, tri3, tl)
    q_l, v_l, sg_l, kk_l, cum_l = _hg_proj(x_lat, n1, sc1, sh1, w_in, lb, tri3, tm)
    zero = jnp.zeros((2, B, HG_HEADS, HG_DK, HG_DK), F32)
    _, _, s_ctx = _hg_scan(q_c, v_c, kk_c, cum_c, tri, zero)
    o_f, o_b, _ = _hg_scan(q_l, v_l, kk_l, cum_l, tri, s_ctx)
    ffn = (hg_w_o[0].astype(BF16), ffn_w_in[1].astype(BF16), ffn_w_out[1].astype(BF16))
    gn = row(jnp.tile(hg_norm_g[0], HG_HEADS))
    return _mix_ffn(_hg_mix_ffn_kernel, "hg_mix_ffn", (o_f, o_b, sg_l, gn),
                    (_tok_spec(tm), _tok_spec(tm), _tok_spec(tm), _resident((1, D_MODEL))),
                    x_lat, g1, row(norm2_g[1]), sc2, sh2, g2, *ffn, tm)
```

```python
import functools

import jax
import jax.numpy as jnp
import numpy as np
from jax import lax
from jax.experimental import pallas as pl
from jax.experimental.pallas import tpu as pltpu

F32 = jnp.float32
BF16 = jnp.bfloat16

D_MODEL = 1024
EPS = 1e-6
GRID_W = 64
WIN_H = 8
WIN_W = 16
NA_HEADS = 16
NA_HEAD_DIM = D_MODEL // NA_HEADS
HG_HEADS = 8
HG_DK = 128
D_FF = 2816
N_MOD = 6

LANES = 128
Q_ROWS = 4
Q_BLK = Q_ROWS * GRID_W
KEY_SLOTS = 3
ATTN_GROUP = 4
GROUP_W = ATTN_GROUP * NA_HEAD_DIM
BOUNDED_SPREAD = 100.0
Q_BLOCKS_PER_STEP = 8
LOG2E = 1.4426950408889634
MASKED = -1e30
SCAN_CHUNK = 64
SCAN_SUB = 8
FACTOR_RANGE = 100.0
GATE_FLOOR = 2.0 ** -100
VMEM_LIMIT = 56 * 1024 * 1024


def _dot(a, b):
    return jnp.dot(a, b, preferred_element_type=F32)


def _dot_nt(a, b):
    return lax.dot_general(a, b, (((1,), (1,)), ((), ())), preferred_element_type=F32)


def _dot_tn(a, b):
    return lax.dot_general(a, b, (((0,), (0,)), ((), ())), preferred_element_type=F32)


def _silu(x):
    half = 0.5 * x
    return half + half * jnp.tanh(half)


def _norm_mod(xf, g, sc, sh):
    ms = jnp.mean(xf * xf, axis=-1, keepdims=True)
    return (xf * lax.rsqrt(ms + EPS) * g) * (1.0 + sc) + sh


def _split_bf16(x):
    hi = x.astype(BF16)
    lo = (x - hi.astype(F32)).astype(BF16)
    return hi, lo


def _resident(shape):
    zeros = (0,) * len(shape)
    return pl.BlockSpec(shape, lambda *_: zeros, pipeline_mode=pl.Buffered(1))


def _mod_spec(arr):
    if arr.shape[0] == 1:
        return pl.BlockSpec((None, 1, D_MODEL), lambda b, i: (0, 0, 0))
    return pl.BlockSpec((None, 1, D_MODEL), lambda b, i: (b, 0, 0))


def _tok_spec(tm, width=D_MODEL):
    return pl.BlockSpec((None, tm, width), lambda b, i: (b, i, 0))


def _params(*sem):
    return pltpu.CompilerParams(dimension_semantics=sem, vmem_limit_bytes=VMEM_LIMIT)


def _ada_kernel(c_ref, w_ref, b_ref, o_ref):
    s = _silu(c_ref[...]).astype(BF16)
    o_ref[...] = _dot(s, w_ref[...].astype(BF16)) + b_ref[...]


def _ada(cvec, ada_w, ada_b):
    depth, _, n = ada_w.shape
    tn = 1536
    return pl.pallas_call(
        _ada_kernel,
        grid=(depth, n // tn),
        in_specs=[pl.BlockSpec((8, D_MODEL), lambda l, j: (0, 0)),
                  pl.BlockSpec((None, D_MODEL, tn), lambda l, j: (l, 0, j)),
                  pl.BlockSpec((None, 1, tn), lambda l, j: (l, 0, j))],
        out_specs=pl.BlockSpec((None, 8, tn), lambda l, j: (l, 0, j)),
        out_shape=jax.ShapeDtypeStruct((depth, 8, n), F32),
        compiler_params=_params("arbitrary", "arbitrary"),
        name="ada",
    )(cvec, ada_w, ada_b.reshape(depth, 1, n))


def _qkv_kernel(x_ref, g_ref, sc_ref, sh_ref, w_ref, wvt_ref, gq_ref, gk_ref, red_ref, exp_ref,
                q_ref, k_ref, vt_ref):
    h = _norm_mod(x_ref[...], g_ref[...], sc_ref[...], sh_ref[...]).astype(BF16)

    def head_norm(z, gain, out_ref):
        ss = _dot((z * z).astype(BF16), red_ref[...])
        r = lax.rsqrt(ss * (1.0 / NA_HEAD_DIM) + EPS)
        rex = _dot(jnp.concatenate(_split_bf16(r), axis=1), exp_ref[...])
        zn = (z * rex * gain).astype(BF16)
        for g in range(D_MODEL // GROUP_W):
            out_ref[g] = zn[:, g * GROUP_W:(g + 1) * GROUP_W]

    q = _dot(h, w_ref[:, 0:D_MODEL])
    k = _dot(h, w_ref[:, D_MODEL:2 * D_MODEL])
    head_norm(q, gq_ref[...], q_ref)
    vt = _dot_nt(wvt_ref[...], h).astype(BF16)
    head_norm(k, gk_ref[...], k_ref)
    for j in range(vt_ref.shape[0]):
        vt_ref[j] = vt[:, j * Q_BLK:(j + 1) * Q_BLK]


def _qkv(x, g, sc, sh, w_qk, w_vt, gq, gk, red, expand, tm):
    B, T, _ = x.shape
    n_grp = D_MODEL // GROUP_W
    grp = jax.ShapeDtypeStruct((n_grp, B, T, GROUP_W), BF16)
    grp_spec = pl.BlockSpec((n_grp, None, tm, GROUP_W), lambda b, i: (0, b, i, 0))
    return pl.pallas_call(
        _qkv_kernel,
        grid=(B, T // tm),
        in_specs=[_tok_spec(tm), _resident((1, D_MODEL)), _mod_spec(sc), _mod_spec(sh),
                  _resident(w_qk.shape), _resident(w_vt.shape), _resident((1, D_MODEL)), _resident((1, D_MODEL)),
                  _resident(red.shape), _resident(expand.shape)],
        out_specs=[grp_spec, grp_spec,
                   pl.BlockSpec((None, tm // Q_BLK, D_MODEL, Q_BLK), lambda b, i: (b, i, 0, 0))],
        out_shape=[grp, grp, jax.ShapeDtypeStruct((B, T // Q_BLK, D_MODEL, Q_BLK), BF16)],
        compiler_params=_params("parallel", "parallel"),
        name="qkv",
    )(x, g, sc, sh, w_qk, w_vt, gq, gk, red, expand)


def _attend(tasks):
    def stage_scores(task):
        q_h, keys, _, biases = task
        s = [_dot_nt(key(), q_h) + bias() for key, bias in zip(keys, biases)]
        m = functools.reduce(jnp.maximum, [jnp.max(t, axis=0, keepdims=True) for t in s])
        return s, m

    def stage_probs(sm):
        s, m = sm
        return [jnp.exp2(t - m).astype(BF16) for t in s]

    n = len(tasks)
    scores, probs, outs = {}, {}, []
    for step in range(n + 2):
        if step < n:
            scores[step] = stage_scores(tasks[step])
        if 0 <= step - 1 < n:
            probs[step - 1] = stage_probs(scores.pop(step - 1))
        if 0 <= step - 2 < n:
            outs.append(_weighted_values(tasks[step - 2], probs.pop(step - 2)))
    return outs


def _attend_bounded(tasks):
    def stage_probs(task):
        q_h, keys, _, biases = task
        return [jnp.exp2(_dot_nt(key(), q_h) + bias()).astype(BF16) for key, bias in zip(keys, biases)]

    n = len(tasks)
    probs, outs = {}, []
    for step in range(n + 1):
        if step < n:
            probs[step] = stage_probs(tasks[step])
        if step >= 1:
            outs.append(_weighted_values(tasks[step - 1], probs.pop(step - 1)))
    return outs


def _weighted_values(task, p):
    ones_rows = 16
    acc = None
    for vt, t in zip(task[2], p):
        v1 = jnp.concatenate([vt(), jnp.ones((ones_rows, t.shape[0]), BF16)], axis=0)
        part = _dot(v1, t)
        acc = part if acc is None else acc + part
    return acc[:NA_HEAD_DIM] * (1.0 / acc[NA_HEAD_DIM:NA_HEAD_DIM + 1])


def _head_queries(q_grp):
    lane_head = lax.broadcasted_iota(jnp.int32, (1, q_grp.shape[1]), 1) // NA_HEAD_DIM
    return [jnp.where(lane_head == h, q_grp, jnp.zeros_like(q_grp)) for h in range(ATTN_GROUP)]


def _head_rows(ref, h):
    return lambda: ref[h * NA_HEAD_DIM:(h + 1) * NA_HEAD_DIM, :]


def _attn_kernel(shift_ref, q_ref, *refs):
    n_halo = Q_BLOCKS_PER_STEP + KEY_SLOTS - 1
    k_refs, v_refs = refs[:n_halo], refs[n_halo:2 * n_halo]
    kc_ref, vc_ref = refs[2 * n_halo:2 * n_halo + 2]
    bias_first, bias_mid, bias_last, o_ref = refs[2 * n_halo + 2:]
    bias_refs = [bias_first] + [bias_mid] * (Q_BLOCKS_PER_STEP - 2) + [bias_last]
    n_ctx = kc_ref.shape[0] // Q_BLK
    shift = shift_ref[0]
    tasks = []
    for jb in range(Q_BLOCKS_PER_STEP):
        q_heads = _head_queries(q_ref[jb * Q_BLK:(jb + 1) * Q_BLK, :])
        for h in range(ATTN_GROUP):
            keys = ([(lambda r=r: r[...]) for r in k_refs[jb:jb + KEY_SLOTS]]
                    + [(lambda t=t: kc_ref[t * Q_BLK:(t + 1) * Q_BLK, :]) for t in range(n_ctx)])
            vts = ([_head_rows(r, h) for r in v_refs[jb:jb + KEY_SLOTS]]
                   + [_head_rows(vc_ref.at[t], h) for t in range(n_ctx)])
            biases = ([(lambda t=t, jb=jb, h=h: bias_refs[jb][h, t * Q_BLK:(t + 1) * Q_BLK, :])
                       for t in range(KEY_SLOTS)] + [lambda: shift] * n_ctx)
            tasks.append((q_heads[h], keys, vts, biases))

    def finish(outs):
        for jb in range(Q_BLOCKS_PER_STEP):
            o_t = jnp.concatenate(outs[jb * ATTN_GROUP:(jb + 1) * ATTN_GROUP], axis=0)
            o_ref[jb * Q_BLK:(jb + 1) * Q_BLK, :] = o_t.T.astype(BF16)

    bounded = shift_ref[1] > 0.5

    @pl.when(bounded)
    def _():
        finish(_attend_bounded(tasks))

    @pl.when(jnp.logical_not(bounded))
    def _():
        finish(_attend(tasks))


def _attn(shift, q, k, vt, kc, vtc, bias):
    n_grp, B, T, gw = q.shape
    L = kc.shape[2]
    nb = T // Q_BLK
    per = Q_BLOCKS_PER_STEP
    n_halo = per + KEY_SLOTS - 1

    def blk(i, d):
        return jnp.clip(per * i + d, 0, nb - 1)

    def variant(qb):
        return jnp.where(qb == 0, 0, jnp.where(qb == nb - 1, 2, 1))

    halo = [pl.BlockSpec((None, None, Q_BLK, gw), lambda g, b, i, d=d: (g, b, blk(i, d - 1), 0))
            for d in range(n_halo)]
    halo_t = [pl.BlockSpec((None, None, gw, Q_BLK), lambda g, b, i, d=d: (b, blk(i, d - 1), g, 0))
              for d in range(n_halo)]
    biases = [pl.BlockSpec((None, ATTN_GROUP, KEY_SLOTS * Q_BLK, Q_BLK), index)
              for index in (lambda g, b, i: (variant(per * i), g, 0, 0),
                            lambda g, b, i: (1, g, 0, 0),
                            lambda g, b, i: (variant(per * i + per - 1), g, 0, 0))]
    tok = pl.BlockSpec((None, None, per * Q_BLK, gw), lambda g, b, i: (g, b, i, 0))
    return pl.pallas_call(
        _attn_kernel,
        grid=(n_grp, B, nb // per),
        in_specs=[pl.BlockSpec(memory_space=pltpu.SMEM), tok] + halo + halo_t
                 + [pl.BlockSpec((None, None, L, gw), lambda g, b, i: (g, b, 0, 0)),
                    pl.BlockSpec((None, L // Q_BLK, gw, Q_BLK), lambda g, b, i: (b, 0, g, 0))] + biases,
        out_specs=tok,
        out_shape=jax.ShapeDtypeStruct((n_grp, B, T, gw), BF16),
        compiler_params=_params("parallel", "parallel", "parallel"),
        name="attn",
    )(shift, q, *([k] * n_halo), *([vt] * n_halo), kc, vtc, bias, bias, bias)


def _attn_ctx_kernel(q_ref, k_ref, vt_ref, o_ref):
    n_ctx = k_ref.shape[1] // Q_BLK
    tasks = []
    for g in range(q_ref.shape[0]):
        q_heads = _head_queries(q_ref[g])
        for h in range(ATTN_GROUP):
            rows = slice(g * GROUP_W + h * NA_HEAD_DIM, g * GROUP_W + (h + 1) * NA_HEAD_DIM)
            tasks.append((q_heads[h],
                          [(lambda t=t, g=g: k_ref[g, t * Q_BLK:(t + 1) * Q_BLK, :]) for t in range(n_ctx)],
                          [(lambda t=t, rows=rows: vt_ref[t, rows, :]) for t in range(n_ctx)],
                          [lambda: 0.0] * n_ctx))
    outs = _attend(tasks)
    for g in range(q_ref.shape[0]):
        o_ref[g] = jnp.concatenate(outs[g * ATTN_GROUP:(g + 1) * ATTN_GROUP], axis=0).T.astype(BF16)


def _attn_ctx(q, k, vt):
    n_grp, B, L, gw = q.shape
    spec = pl.BlockSpec((n_grp, None, L, gw), lambda b: (0, b, 0, 0))
    return pl.pallas_call(
        _attn_ctx_kernel,
        grid=(B,),
        in_specs=[spec, spec, pl.BlockSpec((None, L // Q_BLK, D_MODEL, Q_BLK), lambda b: (b, 0, 0, 0))],
        out_specs=spec,
        out_shape=jax.ShapeDtypeStruct((n_grp, B, L, gw), BF16),
        compiler_params=_params("parallel"),
        name="attn_ctx",
    )(q, k, vt)


def _attn_bias(rpb, rows, shift):
    n_heads = rpb.shape[0]
    n_slot = KEY_SLOTS * Q_ROWS
    cc = np.arange(GRID_W)[:, None]
    c = np.arange(GRID_W)[None, :]
    onehot = (cc - c + WIN_W - 1)[None] == np.arange(2 * WIN_W - 1)[:, None, None]
    toep = jnp.einsum('hdk,kxc->hdxc', rpb, jnp.asarray(onehot, F32), precision=lax.Precision.HIGHEST)
    col_start = np.clip(c - WIN_W // 2, 0, GRID_W - WIN_W)
    toep = jnp.where((cc >= col_start) & (cc < col_start + WIN_W), toep + shift, MASKED)
    lo_d = WIN_H - 1 - Q_ROWS
    bias = jnp.concatenate([toep[:, lo_d - a:lo_d - a + n_slot] for a in range(Q_ROWS)], axis=-1)
    full = (n_slot, GRID_W, Q_ROWS, GRID_W)
    j = np.arange(n_slot)[:, None, None, None]
    a = np.arange(Q_ROWS)[None, None, :, None]
    kh = min(WIN_H, rows)
    lows = (np.full_like(a, Q_ROWS), a + Q_ROWS - kh // 2, np.full_like(a, 2 * Q_ROWS - kh))
    row_mask = np.stack([np.where(np.broadcast_to((j >= lo) & (j < lo + kh), full), 0.0, MASKED) for lo in lows])
    return (bias.reshape(n_heads, KEY_SLOTS * Q_BLK, Q_BLK),
            jnp.asarray(row_mask.reshape(3, KEY_SLOTS * Q_BLK, Q_BLK), F32))


MXU_DEPTH = 256
FF_BOUNDS = (0, MXU_DEPTH * (D_FF // MXU_DEPTH + 1) // 2, D_FF)


def _mix_ffn_tail(mix, x_ref, g1_ref, n2_ref, sc2_ref, sh2_ref, g2_ref, wo_ref, win_ref, wout_ref, o_ref):
    tm = x_ref.shape[0]
    n_sub = 2 if tm % (2 * Q_BLK) == 0 else 1
    halves = [slice(i * (tm // n_sub), (i + 1) * (tm // n_sub)) for i in range(n_sub)]

    def head(rows):
        x1 = x_ref[rows, :] + g1_ref[...] * mix(wo_ref, rows)
        return x1, _norm_mod(x1, n2_ref[...], sc2_ref[...], sh2_ref[...]).astype(BF16)

    def ffn(rows, x1, h2):
        acc = None
        for lo, hi in zip(FF_BOUNDS[:-1], FF_BOUNDS[1:]):
            gate = _dot(h2, win_ref[:, lo:hi])
            up = _dot(h2, win_ref[:, D_FF + lo:D_FF + hi])
            t = _dot((_silu(gate) * up).astype(BF16), wout_ref[lo:hi, :])
            acc = t if acc is None else acc + t
        o_ref[rows, :] = x1 + g2_ref[...] * acc

    heads = [head(rows) for rows in halves]
    for rows, (x1, h2) in zip(halves, heads):
        ffn(rows, x1, h2)


def _na_mix_ffn_kernel(a_ref, *rest):
    def mix(wo_ref, rows):
        parts = [_dot(a_ref[g, rows, :], wo_ref[g * GROUP_W:(g + 1) * GROUP_W, :]) for g in range(a_ref.shape[0])]
        return functools.reduce(jnp.add, parts)

    _mix_ffn_tail(mix, *rest)


def _hg_mix_ffn_kernel(of_ref, ob_ref, sg_ref, gn_ref, *rest):
    def mix(wo_ref, rows):
        o = of_ref[rows, :].astype(F32) + ob_ref[rows, :].astype(F32)
        parts = []
        for h in range(HG_HEADS):
            oh = o[:, h * HG_DK:(h + 1) * HG_DK]
            ms = jnp.mean(oh * oh, axis=-1, keepdims=True)
            parts.append(oh * lax.rsqrt(ms + EPS))
        a = (jnp.concatenate(parts, axis=1) * gn_ref[...] * sg_ref[rows, :].astype(F32)).astype(BF16)
        return _dot(a, wo_ref[...])

    _mix_ffn_tail(mix, *rest)


def _mix_ffn(kernel, name, lead, lead_specs, x, g1, n2, sc2, sh2, g2, wo, win, wout, tm):
    B, T, _ = x.shape
    return pl.pallas_call(
        kernel,
        grid=(B, T // tm),
        in_specs=list(lead_specs) + [
            _tok_spec(tm), _mod_spec(g1), _resident((1, D_MODEL)), _mod_spec(sc2), _mod_spec(sh2), _mod_spec(g2),
            _resident(wo.shape), _resident(win.shape), _resident(wout.shape)],
        out_specs=_tok_spec(tm),
        out_shape=jax.ShapeDtypeStruct((B, T, D_MODEL), F32),
        compiler_params=_params("parallel", "parallel"),
        name=name,
    )(*lead, x, g1, n2, sc2, sh2, g2, wo, win, wout)


def _hg_proj_kernel(x_ref, g_ref, sc_ref, sh_ref, w_ref, lb_ref, tri3_ref, q_ref, v_ref, sg_ref, kk_ref, cum_ref):
    halves = [slice(0, x_ref.shape[0])]
    hs = [_norm_mod(x_ref[rows, :], g_ref[...], sc_ref[...], sh_ref[...]).astype(BF16) for rows in halves]

    def gates(d, rows, f_pre):
        lb = lb_ref[d]
        half = 0.5 * (1.0 - lb)
        th = jnp.tanh(0.5 * f_pre)
        f = jnp.maximum(0.5 * (1.0 + lb) + half * th, GATE_FLOOR)
        kk_ref[d, rows, :] = (half - half * th).astype(BF16)
        lf = jnp.log2(f)
        hi = lf.astype(BF16)
        mid, lo = _split_bf16(lf - hi.astype(F32))
        for c in range(f_pre.shape[0] // SCAN_CHUNK):
            sub = slice(c * SCAN_CHUNK, (c + 1) * SCAN_CHUNK)
            parts = jnp.concatenate([hi[sub], mid[sub], lo[sub]], axis=0)
            cum_ref[d, rows.start + sub.start:rows.start + sub.stop, :] = _dot(tri3_ref[d], parts)

    def store(ref, fn, rows, z):
        ref[rows, :] = fn(z).astype(BF16)

    work = []
    for rows, h in zip(halves, hs):
        work += [(h, 3, functools.partial(gates, 0, rows)),
                 (h, 4, functools.partial(gates, 1, rows)),
                 (h, 0, functools.partial(store, q_ref, _silu, rows)),
                 (h, 2, functools.partial(store, sg_ref, _silu, rows)),
                 (h, 1, functools.partial(store, v_ref, lambda z: z, rows))]

    def project(item):
        h, n, _ = item
        return _dot(h, w_ref[:, n * D_MODEL:(n + 1) * D_MODEL])

    pending = project(work[0])
    for n, item in enumerate(work):
        z = pending
        if n + 1 < len(work):
            pending = project(work[n + 1])
        item[2](z)


def _hg_proj(x, g, sc, sh, w, lb, tri3, tm):
    B, T, _ = x.shape
    tok = jax.ShapeDtypeStruct((B, T, D_MODEL), BF16)
    two = pl.BlockSpec((2, None, tm, D_MODEL), lambda b, i: (0, b, i, 0))
    return pl.pallas_call(
        _hg_proj_kernel,
        grid=(B, T // tm),
        in_specs=[_tok_spec(tm), _resident((1, D_MODEL)), _mod_spec(sc), _mod_spec(sh),
                  _resident(w.shape), _resident(lb.shape), _resident(tri3.shape)],
        out_specs=[_tok_spec(tm)] * 3 + [two, two],
        out_shape=[tok] * 3 + [jax.ShapeDtypeStruct((2, B, T, D_MODEL), BF16),
                               jax.ShapeDtypeStruct((2, B, T, D_MODEL), F32)],
        compiler_params=_params("parallel", "parallel"),
        name="hg_proj",
    )(x, g, sc, sh, w, lb, tri3)


def _hg_scan_kernel(qf_ref, vf_ref, kkf_ref, cumf_ref, qb_ref, vb_ref, kkb_ref, cumb_ref, tri_ref, s0_ref,
                    of_ref, ob_ref, st_ref, a_scr):
    @pl.when(pl.program_id(1) == 0)
    def _():
        st_ref[...] = s0_ref[...]

    C = SCAN_CHUNK
    n_sub = qf_ref.shape[0] // C
    dirs = ((qf_ref, vf_ref, kkf_ref, cumf_ref, of_ref), (qb_ref, vb_ref, kkb_ref, cumb_ref, ob_ref))
    streams = [(d, slice((j if d == 0 else n_sub - 1 - j) * C, (j if d == 0 else n_sub - 1 - j) * C + C))
               for j in range(n_sub) for d in range(2)]
    pair_w = 2 * HG_DK
    pairs = [slice(p * pair_w, (p + 1) * pair_w) for p in range(HG_HEADS // 2)]
    first = lax.broadcasted_iota(jnp.int32, (1, pair_w), 1) < HG_DK

    def block_diag(x):
        zero = jnp.zeros_like(x)
        return jnp.concatenate([jnp.where(first, x, zero), jnp.where(first, zero, x)], axis=0)

    def chunk_total(d, rows):
        last = rows.stop - 1 if d == 0 else rows.start
        return dirs[d][3][last:last + 1, :]

    out_of_range = [jnp.max(-0.5 * chunk_total(d, rows)) > FACTOR_RANGE for d, rows in streams]

    def stage_prep(n):
        d, rows = streams[n]
        q_ref, _, kk_ref, cum_ref, _ = dirs[d]
        b = cum_ref[rows, :]
        tot = chunk_total(d, rows)
        mid = 0.5 * tot
        q = q_ref[rows, :]
        kk = kk_ref[rows, :]
        return dict(q_in=q * jnp.exp2(b - mid).astype(BF16), k_in=kk * jnp.exp2(mid - b).astype(BF16),
                    q_st=q * jnp.exp2(b).astype(BF16), k_st=kk * jnp.exp2(tot - b).astype(BF16),
                    decay=jnp.exp2(tot))

    def stage_intra(n, p, check_range):
        d, rows = streams[n]
        seen = jnp.concatenate([tri_ref[d]] * 2, axis=1) > 0.0
        a_s = [jnp.where(seen, _dot_nt(p["q_in"][:, pr], block_diag(p["k_in"][:, pr])), 0.0).astype(BF16)
               for pr in pairs]
        if not check_range:
            return a_s
        a_scr[n] = jnp.concatenate(a_s, axis=1)

        @pl.when(out_of_range[n])
        def _():
            a_scr[n] = _pairwise_intra(dirs[d][0][rows, :].astype(F32), dirs[d][2][rows, :].astype(F32),
                                       dirs[d][3], rows, d == 0).astype(BF16)

        return [a_scr[n, :, i * 2 * C:(i + 1) * 2 * C] for i in range(len(pairs))]

    def stage_values(n, p, a_s):
        d, rows = streams[n]
        v_ref = dirs[d][1]
        o_intra = [_dot(a, block_diag(v_ref[rows, pr])) for a, pr in zip(a_s, pairs)]
        update = [_dot_tn(v_ref[rows, h * HG_DK:(h + 1) * HG_DK], p["k_st"][:, h * HG_DK:(h + 1) * HG_DK])
                  for h in range(HG_HEADS)]
        return o_intra, update

    def stage_state(n, p, o_intra, update):
        d, rows = streams[n]
        zero = jnp.zeros((HG_DK, HG_DK), BF16)
        outs = []
        for i, pr in enumerate(pairs):
            st_a, st_b = st_ref[d, 2 * i], st_ref[d, 2 * i + 1]
            st_bd = jnp.concatenate([jnp.concatenate([st_a.astype(BF16), zero], axis=1),
                                     jnp.concatenate([zero, st_b.astype(BF16)], axis=1)], axis=0)
            outs.append(o_intra[i] + _dot_nt(p["q_st"][:, pr], st_bd))
            st_ref[d, 2 * i] = st_a * p["decay"][:, pr][:, :HG_DK] + update[2 * i]
            st_ref[d, 2 * i + 1] = st_b * p["decay"][:, pr][:, HG_DK:] + update[2 * i + 1]
        dirs[d][4][rows, :] = jnp.concatenate(outs, axis=1).astype(BF16)

    def run(check_range):
        n_str = len(streams)
        prep, intra, vals = {}, {}, {}
        for t in range(n_str + 3):
            if t < n_str:
                prep[t] = stage_prep(t)
            if 0 <= t - 1 < n_str:
                intra[t - 1] = stage_intra(t - 1, prep[t - 1], check_range)
            if 0 <= t - 2 < n_str:
                vals[t - 2] = stage_values(t - 2, prep[t - 2], intra.pop(t - 2))
            if 0 <= t - 3 < n_str:
                stage_state(t - 3, prep.pop(t - 3), *vals.pop(t - 3))

    any_out = functools.reduce(jnp.logical_or, out_of_range)
    pl.when(jnp.logical_not(any_out))(lambda: run(False))
    pl.when(any_out)(lambda: run(True))


def _pairwise_intra(q, kk, cum_ref, rows, forward):
    C = q.shape[0]
    b = cum_ref[rows, :]
    t_idx = lax.broadcasted_iota(jnp.int32, (C, 1), 0)
    lane = lax.broadcasted_iota(jnp.int32, (1, HG_HEADS * C), 1)

    def body(s, acc):
        b_s = cum_ref[pl.ds(rows.start + s, 1), :]
        k_s = jnp.sum(jnp.where(t_idx == s, kk, 0.0), axis=0, keepdims=True)
        w = q * (k_s * jnp.exp2(jnp.minimum(b - b_s, 0.0)))
        seen = (t_idx >= s) if forward else (t_idx <= s)
        for h in range(HG_HEADS):
            col = jnp.sum(w[:, h * HG_DK:(h + 1) * HG_DK], axis=1, keepdims=True)
            acc = jnp.where(lane == h * C + s, jnp.where(seen, col, 0.0), acc)
        return acc

    return lax.fori_loop(0, C, body, jnp.zeros((C, HG_HEADS * C), F32))


def _hg_scan(q, v, kk, cum, tri, s0):
    B, T, _ = q.shape
    C = SCAN_CHUNK * max(s for s in range(1, SCAN_SUB + 1) if T % (SCAN_CHUNK * s) == 0)
    nc = T // C
    fwd = pl.BlockSpec((None, C, D_MODEL), lambda b, i: (b, i, 0))
    bwd = pl.BlockSpec((None, C, D_MODEL), lambda b, i: (b, nc - 1 - i, 0))
    dfwd = pl.BlockSpec((None, None, C, D_MODEL), lambda b, i: (0, b, i, 0))
    dbwd = pl.BlockSpec((None, None, C, D_MODEL), lambda b, i: (1, b, nc - 1 - i, 0))
    state = pl.BlockSpec((2, None, HG_HEADS, HG_DK, HG_DK), lambda b, i: (0, b, 0, 0, 0))
    tok = jax.ShapeDtypeStruct((B, T, D_MODEL), BF16)
    return pl.pallas_call(
        _hg_scan_kernel,
        grid=(B, nc),
        in_specs=[fwd, fwd, dfwd, dfwd, bwd, bwd, dbwd, dbwd, _resident(tri.shape), state],
        out_specs=[fwd, bwd, state],
        out_shape=[tok, tok, jax.ShapeDtypeStruct((2, B, HG_HEADS, HG_DK, HG_DK), F32)],
        scratch_shapes=[pltpu.VMEM((2 * C // SCAN_CHUNK, SCAN_CHUNK, HG_HEADS * SCAN_CHUNK), BF16)],
        compiler_params=_params("parallel", "arbitrary"),
        name="hg_scan",
    )(q, v, kk, cum, q, v, kk, cum, tri, s0)


def _tile(n, pref):
    return pref if n % pref == 0 else n


def kernel(x, c, ctx, c_ctx, ada_w, ada_b, norm1_g, norm2_g, na_w_qkv, na_w_o, na_q_gain, na_k_gain, na_rpb,
           hg_w_in, hg_lower, hg_norm_g, hg_w_o, ffn_w_in, ffn_w_out):
    B, T, _ = x.shape
    L = ctx.shape[1]
    rows = T // GRID_W
    assert B + 1 <= 8 and T % (Q_BLOCKS_PER_STEP * Q_BLK) == 0 and rows >= WIN_H
    assert T % SCAN_CHUNK == 0 and L % Q_BLK == 0
    tm = _tile(T, 512)
    tl = _tile(B * L, 512)
    row = lambda a: a.reshape(1, -1).astype(F32)

    cvec = jnp.zeros((8, D_MODEL), F32).at[:B].set(c).at[B].set(c_ctx)
    mods = _ada(cvec, ada_w, ada_b)

    def mod_rows(i):
        lat = [m.reshape(B, 1, D_MODEL) for m in jnp.split(mods[i, :B], N_MOD, axis=-1)]
        cx = [m.reshape(1, 1, D_MODEL) for m in jnp.split(mods[i, B], N_MOD, axis=-1)]
        return lat, cx

    (sh1, sc1, g1, sh2, sc2, g2), (csh1, csc1, cg1, csh2, csc2, cg2) = mod_rows(0)
    w_qk = na_w_qkv[0, :, :2 * D_MODEL].astype(BF16)
    w_vt = na_w_qkv[0, :, 2 * D_MODEL:].T.astype(BF16)
    gq = row(jnp.tile(na_q_gain[0], NA_HEADS) * (NA_HEAD_DIM ** -0.5 * LOG2E))
    gk = row(jnp.tile(na_k_gain[0], NA_HEADS))
    head_of = np.arange(D_MODEL) // NA_HEAD_DIM
    red = jnp.asarray(head_of[:, None] == np.arange(LANES)[None, :], BF16)
    expand = jnp.asarray(np.arange(2 * LANES)[:, None] % LANES == head_of[None, :], BF16)
    n1 = row(norm1_g[0])
    q_l, k_l, vt_l = _qkv(x, n1, sc1, sh1, w_qk, w_vt, gq, gk, red, expand, tm)
    ctx_flat = ctx.reshape(1, B * L, D_MODEL)
    n_grp = D_MODEL // GROUP_W
    q_c, k_c, vt_c = _qkv(ctx_flat, n1, csc1, csh1, w_qk, w_vt, gq, gk, red, expand, tl)
    q_c, k_c = q_c.reshape(n_grp, B, L, GROUP_W), k_c.reshape(n_grp, B, L, GROUP_W)
    vt_c = vt_c.reshape(B, L // Q_BLK, D_MODEL, Q_BLK)
    rpb2 = na_rpb[0].astype(F32) * LOG2E
    qk_max = NA_HEAD_DIM * jnp.max(jnp.abs(gq)) * jnp.max(jnp.abs(gk))
    bound = qk_max + jnp.maximum(jnp.max(rpb2), 0.0)
    shift = jnp.stack([-bound, (bound + qk_max < BOUNDED_SPREAD).astype(F32)])
    bias, row_mask = _attn_bias(rpb2, rows, -bound)
    a_l = _attn(shift, q_l, k_l, vt_l, k_c, vt_c, bias[None] + row_mask[:, None])
    a_c = _attn_ctx(q_c, k_c, vt_c)
    ffn = (na_w_o[0].astype(BF16), ffn_w_in[0].astype(BF16), ffn_w_out[0].astype(BF16))
    n2 = row(norm2_g[0])
    grp_spec = lambda t: pl.BlockSpec((D_MODEL // GROUP_W, None, t, GROUP_W), lambda b, i: (0, b, i, 0))
    x_lat = _mix_ffn(_na_mix_ffn_kernel, "na_mix_ffn", (a_l,), (grp_spec(tm),),
                     x, g1, n2, sc2, sh2, g2, *ffn, tm)
    x_ctx = _mix_ffn(_na_mix_ffn_kernel, "na_mix_ffn_ctx", (a_c.reshape(n_grp, 1, B * L, GROUP_W),), (grp_spec(tl),),
                     ctx_flat, cg1, n2, csc2, csh2, cg2, *ffn, tl)

    (sh1, sc1, g1, sh2, sc2, g2), (csh1, csc1, _, _, _, _) = mod_rows(1)
    lbs = jnp.cumsum(jax.nn.softmax(hg_lower.astype(F32), axis=0), axis=0)
    lb = (lbs - lbs[:1])[1].reshape(2, 1, D_MODEL)
    w_in = hg_w_in[0].astype(BF16)
    n1 = row(norm1_g[1])
    t_idx = np.arange(SCAN_CHUNK)
    tri_np = np.stack([t_idx[None, :] <= t_idx[:, None], t_idx[None, :] >= t_idx[:, None]])
    tri = jnp.asarray(tri_np, F32)
    tri3 = jnp.asarray(np.tile(tri_np, (1, 1, 3)), BF16)
    q_c, v_c, _, kk_c, cum_c = _hg_proj(x_ctx, n1, csc1, csh1, w_in, lb, tri3, tl)
    q_c, v_c = q_c.reshape(B, L, D_MODEL), v_c.reshape(B, L, D_MODEL)
    kk_c, cum_c = kk_c.reshape(2, B, L, D_MODEL), cum_c.reshape(2, B, L, D_MODEL)
    q_l, v_l, sg_l, kk_l, cum_l = _hg_proj(x_lat, n1, sc1, sh1, w_in, lb, tri3, tm)
    zero = jnp.zeros((2, B, HG_HEADS, HG_DK, HG_DK), F32)
    _, _, s_ctx = _hg_scan(q_c, v_c, kk_c, cum_c, tri, zero)
    o_f, o_b, _ = _hg_scan(q_l, v_l, kk_l, cum_l, tri, s_ctx)
    ffn = (hg_w_o[0].astype(BF16), ffn_w_in[1].astype(BF16), ffn_w_out[1].astype(BF16))
    gn = row(jnp.tile(hg_norm_g[0], HG_HEADS))
    return _mix_ffn(_hg_mix_ffn_kernel, "hg_mix_ffn", (o_f, o_b, sg_l, gn),
                    (_tok_spec(tm), _tok_spec(tm), _tok_spec(tm), _resident((1, D_MODEL))),
                    x_lat, g1, row(norm2_g[1]), sc2, sh2, g2, *ffn, tm)
```

```python
import functools

import jax
import jax.numpy as jnp
import numpy as np
from jax import lax
from jax.experimental import pallas as pl
from jax.experimental.pallas import tpu as pltpu

F32 = jnp.float32
BF16 = jnp.bfloat16

D_MODEL = 1024
EPS = 1e-6
GRID_W = 64
WIN_H = 8
WIN_W = 16
NA_HEADS = 16
NA_HEAD_DIM = D_MODEL // NA_HEADS
HG_HEADS = 8
HG_DK = 128
D_FF = 2816
N_MOD = 6

LANES = 128
Q_ROWS = 4
Q_BLK = Q_ROWS * GRID_W
KEY_SLOTS = 3
ATTN_GROUP = 4
GROUP_W = ATTN_GROUP * NA_HEAD_DIM
BOUNDED_SPREAD = 60.0
Q_BLOCKS_PER_STEP = 8
LOG2E = 1.4426950408889634
MASKED = -1e30
SCAN_CHUNK = 64
SCAN_SUB = 8
FACTOR_RANGE = 100.0
GATE_FLOOR = 2.0 ** -100
VMEM_LIMIT = 56 * 1024 * 1024


def _dot(a, b):
    return jnp.dot(a, b, preferred_element_type=F32)


def _dot_nt(a, b):
    return lax.dot_general(a, b, (((1,), (1,)), ((), ())), preferred_element_type=F32)


def _dot_tn(a, b):
    return lax.dot_general(a, b, (((0,), (0,)), ((), ())), preferred_element_type=F32)


def _silu(x):
    half = 0.5 * x
    return half + half * jnp.tanh(half)


def _norm_mod(xf, g, sc, sh):
    ms = jnp.mean(xf * xf, axis=-1, keepdims=True)
    return (xf * lax.rsqrt(ms + EPS) * g) * (1.0 + sc) + sh


def _split_bf16(x):
    hi = x.astype(BF16)
    lo = (x - hi.astype(F32)).astype(BF16)
    return hi, lo


def _resident(shape):
    zeros = (0,) * len(shape)
    return pl.BlockSpec(shape, lambda *_: zeros, pipeline_mode=pl.Buffered(1))


def _mod_spec(arr):
    if arr.shape[0] == 1:
        return pl.BlockSpec((None, 1, D_MODEL), lambda b, i: (0, 0, 0))
    return pl.BlockSpec((None, 1, D_MODEL), lambda b, i: (b, 0, 0))


def _tok_spec(tm, width=D_MODEL):
    return pl.BlockSpec((None, tm, width), lambda b, i: (b, i, 0))


def _params(*sem):
    return pltpu.CompilerParams(dimension_semantics=sem, vmem_limit_bytes=VMEM_LIMIT)


def _ada_kernel(c_ref, w_ref, b_ref, o_ref):
    s = _silu(c_ref[...]).astype(BF16)
    o_ref[...] = _dot(s, w_ref[...].astype(BF16)) + b_ref[...]


def _ada(cvec, ada_w, ada_b):
    depth, _, n = ada_w.shape
    tn = 1536
    return pl.pallas_call(
        _ada_kernel,
        grid=(depth, n // tn),
        in_specs=[pl.BlockSpec((8, D_MODEL), lambda l, j: (0, 0)),
                  pl.BlockSpec((None, D_MODEL, tn), lambda l, j: (l, 0, j)),
                  pl.BlockSpec((None, 1, tn), lambda l, j: (l, 0, j))],
        out_specs=pl.BlockSpec((None, 8, tn), lambda l, j: (l, 0, j)),
        out_shape=jax.ShapeDtypeStruct((depth, 8, n), F32),
        compiler_params=_params("arbitrary", "arbitrary"),
        name="ada",
    )(cvec, ada_w, ada_b.reshape(depth, 1, n))


def _qkv_kernel(x_ref, g_ref, sc_ref, sh_ref, w_ref, wvt_ref, gq_ref, gk_ref, red_ref, exp_ref,
                q_ref, k_ref, vt_ref):
    h = _norm_mod(x_ref[...], g_ref[...], sc_ref[...], sh_ref[...]).astype(BF16)

    def head_norm(z, gain, out_ref):
        ss = _dot((z * z).astype(BF16), red_ref[...])
        r = lax.rsqrt(ss * (1.0 / NA_HEAD_DIM) + EPS)
        rex = _dot(jnp.concatenate(_split_bf16(r), axis=1), exp_ref[...])
        zn = (z * rex * gain).astype(BF16)
        for g in range(D_MODEL // GROUP_W):
            out_ref[g] = zn[:, g * GROUP_W:(g + 1) * GROUP_W]

    q = _dot(h, w_ref[:, 0:D_MODEL])
    k = _dot(h, w_ref[:, D_MODEL:2 * D_MODEL])
    head_norm(q, gq_ref[...], q_ref)
    vt = _dot_nt(wvt_ref[...], h).astype(BF16)
    head_norm(k, gk_ref[...], k_ref)
    for j in range(vt_ref.shape[0]):
        vt_ref[j] = vt[:, j * Q_BLK:(j + 1) * Q_BLK]


def _qkv(x, g, sc, sh, w_qk, w_vt, gq, gk, red, expand, tm):
    B, T, _ = x.shape
    n_grp = D_MODEL // GROUP_W
    grp = jax.ShapeDtypeStruct((n_grp, B, T, GROUP_W), BF16)
    grp_spec = pl.BlockSpec((n_grp, None, tm, GROUP_W), lambda b, i: (0, b, i, 0))
    return pl.pallas_call(
        _qkv_kernel,
        grid=(B, T // tm),
        in_specs=[_tok_spec(tm), _resident((1, D_MODEL)), _mod_spec(sc), _mod_spec(sh),
                  _resident(w_qk.shape), _resident(w_vt.shape), _resident((1, D_MODEL)), _resident((1, D_MODEL)),
                  _resident(red.shape), _resident(expand.shape)],
        out_specs=[grp_spec, grp_spec,
                   pl.BlockSpec((None, tm // Q_BLK, D_MODEL, Q_BLK), lambda b, i: (b, i, 0, 0))],
        out_shape=[grp, grp, jax.ShapeDtypeStruct((B, T // Q_BLK, D_MODEL, Q_BLK), BF16)],
        compiler_params=_params("parallel", "parallel"),
        name="qkv",
    )(x, g, sc, sh, w_qk, w_vt, gq, gk, red, expand)


def _attend(tasks):
    def stage_scores(task):
        s = _scores(task)
        m = functools.reduce(jnp.maximum, [jnp.max(t, axis=0, keepdims=True) for t in s])
        return s, m

    def stage_probs(sm):
        s, m = sm
        return [jnp.exp2(t - m).astype(BF16) for t in s]

    n = len(tasks)
    scores, probs, outs = {}, {}, []
    for step in range(n + 2):
        if step < n:
            scores[step] = stage_scores(tasks[step])
        if 0 <= step - 1 < n:
            probs[step - 1] = stage_probs(scores.pop(step - 1))
        if 0 <= step - 2 < n:
            outs.append(_weighted_values(tasks[step - 2], probs.pop(step - 2)))
    return outs


def _scores(task):
    q_h, keys, _, biases = task
    s = []
    for key, bias in zip(keys, biases):
        t = _dot_nt(key(), q_h)
        s.append(t if bias is None else t + bias())
    return s


def _attend_bounded(tasks):
    def stage_probs(task):
        return [jnp.exp2(t).astype(BF16) for t in _scores(task)]

    n = len(tasks)
    probs, outs = {}, []
    for step in range(n + 1):
        if step < n:
            probs[step] = stage_probs(tasks[step])
        if step >= 1:
            outs.append(_weighted_values(tasks[step - 1], probs.pop(step - 1)))
    return outs


def _weighted_values(task, p):
    ones_rows = 16
    acc = None
    for vt, t in zip(task[2], p):
        v1 = jnp.concatenate([vt(), jnp.ones((ones_rows, t.shape[0]), BF16)], axis=0)
        part = _dot(v1, t)
        acc = part if acc is None else acc + part
    return acc[:NA_HEAD_DIM] * (1.0 / acc[NA_HEAD_DIM:NA_HEAD_DIM + 1])


def _head_queries(q_grp):
    lane_head = lax.broadcasted_iota(jnp.int32, (1, q_grp.shape[1]), 1) // NA_HEAD_DIM
    return [jnp.where(lane_head == h, q_grp, jnp.zeros_like(q_grp)) for h in range(ATTN_GROUP)]


def _head_rows(ref, h):
    return lambda: ref[h * NA_HEAD_DIM:(h + 1) * NA_HEAD_DIM, :]


def _attn_kernel(bounded_ref, q_ref, *refs):
    n_halo = Q_BLOCKS_PER_STEP + KEY_SLOTS - 1
    k_refs, v_refs = refs[:n_halo], refs[n_halo:2 * n_halo]
    kc_ref, vc_ref = refs[2 * n_halo:2 * n_halo + 2]
    bias_first, bias_mid, bias_last, o_ref = refs[2 * n_halo + 2:]
    bias_refs = [bias_first] + [bias_mid] * (Q_BLOCKS_PER_STEP - 2) + [bias_last]
    n_ctx = kc_ref.shape[0] // Q_BLK
    tasks = []
    for jb in range(Q_BLOCKS_PER_STEP):
        q_heads = _head_queries(q_ref[jb * Q_BLK:(jb + 1) * Q_BLK, :])
        for h in range(ATTN_GROUP):
            keys = ([(lambda r=r: r[...]) for r in k_refs[jb:jb + KEY_SLOTS]]
                    + [(lambda t=t: kc_ref[t * Q_BLK:(t + 1) * Q_BLK, :]) for t in range(n_ctx)])
            vts = ([_head_rows(r, h) for r in v_refs[jb:jb + KEY_SLOTS]]
                   + [_head_rows(vc_ref.at[t], h) for t in range(n_ctx)])
            biases = ([(lambda t=t, jb=jb, h=h: bias_refs[jb][h, t * Q_BLK:(t + 1) * Q_BLK, :])
                       for t in range(KEY_SLOTS)] + [None] * n_ctx)
            tasks.append((q_heads[h], keys, vts, biases))

    def finish(outs):
        for jb in range(Q_BLOCKS_PER_STEP):
            o_t = jnp.concatenate(outs[jb * ATTN_GROUP:(jb + 1) * ATTN_GROUP], axis=0)
            o_ref[jb * Q_BLK:(jb + 1) * Q_BLK, :] = o_t.T.astype(BF16)

    bounded = bounded_ref[0] > 0.5

    @pl.when(bounded)
    def _():
        finish(_attend_bounded(tasks))

    @pl.when(jnp.logical_not(bounded))
    def _():
        finish(_attend(tasks))


def _attn(bounded, q, k, vt, kc, vtc, bias):
    n_grp, B, T, gw = q.shape
    L = kc.shape[2]
    nb = T // Q_BLK
    per = Q_BLOCKS_PER_STEP
    n_halo = per + KEY_SLOTS - 1

    def blk(i, d):
        return jnp.clip(per * i + d, 0, nb - 1)

    def variant(qb):
        return jnp.where(qb == 0, 0, jnp.where(qb == nb - 1, 2, 1))

    halo = [pl.BlockSpec((None, None, Q_BLK, gw), lambda g, b, i, d=d: (g, b, blk(i, d - 1), 0))
            for d in range(n_halo)]
    halo_t = [pl.BlockSpec((None, None, gw, Q_BLK), lambda g, b, i, d=d: (b, blk(i, d - 1), g, 0))
              for d in range(n_halo)]
    biases = [pl.BlockSpec((None, ATTN_GROUP, KEY_SLOTS * Q_BLK, Q_BLK), index)
              for index in (lambda g, b, i: (variant(per * i), g, 0, 0),
                            lambda g, b, i: (1, g, 0, 0),
                            lambda g, b, i: (variant(per * i + per - 1), g, 0, 0))]
    tok = pl.BlockSpec((None, None, per * Q_BLK, gw), lambda g, b, i: (g, b, i, 0))
    return pl.pallas_call(
        _attn_kernel,
        grid=(n_grp, B, nb // per),
        in_specs=[pl.BlockSpec(memory_space=pltpu.SMEM), tok] + halo + halo_t
                 + [pl.BlockSpec((None, None, L, gw), lambda g, b, i: (g, b, 0, 0)),
                    pl.BlockSpec((None, L // Q_BLK, gw, Q_BLK), lambda g, b, i: (b, 0, g, 0))] + biases,
        out_specs=tok,
        out_shape=jax.ShapeDtypeStruct((n_grp, B, T, gw), BF16),
        compiler_params=_params("parallel", "parallel", "parallel"),
        name="attn",
    )(bounded, q, *([k] * n_halo), *([vt] * n_halo), kc, vtc, bias, bias, bias)


def _attn_ctx_kernel(q_ref, k_ref, vt_ref, o_ref):
    n_ctx = k_ref.shape[1] // Q_BLK
    tasks = []
    for g in range(q_ref.shape[0]):
        q_heads = _head_queries(q_ref[g])
        for h in range(ATTN_GROUP):
            rows = slice(g * GROUP_W + h * NA_HEAD_DIM, g * GROUP_W + (h + 1) * NA_HEAD_DIM)
            tasks.append((q_heads[h],
                          [(lambda t=t, g=g: k_ref[g, t * Q_BLK:(t + 1) * Q_BLK, :]) for t in range(n_ctx)],
                          [(lambda t=t, rows=rows: vt_ref[t, rows, :]) for t in range(n_ctx)],
                          [None] * n_ctx))
    outs = _attend(tasks)
    for g in range(q_ref.shape[0]):
        o_ref[g] = jnp.concatenate(outs[g * ATTN_GROUP:(g + 1) * ATTN_GROUP], axis=0).T.astype(BF16)


def _attn_ctx(q, k, vt):
    n_grp, B, L, gw = q.shape
    spec = pl.BlockSpec((n_grp, None, L, gw), lambda b: (0, b, 0, 0))
    return pl.pallas_call(
        _attn_ctx_kernel,
        grid=(B,),
        in_specs=[spec, spec, pl.BlockSpec((None, L // Q_BLK, D_MODEL, Q_BLK), lambda b: (b, 0, 0, 0))],
        out_specs=spec,
        out_shape=jax.ShapeDtypeStruct((n_grp, B, L, gw), BF16),
        compiler_params=_params("parallel"),
        name="attn_ctx",
    )(q, k, vt)


def _attn_bias(rpb, rows):
    n_heads = rpb.shape[0]
    n_slot = KEY_SLOTS * Q_ROWS
    cc = np.arange(GRID_W)[:, None]
    c = np.arange(GRID_W)[None, :]
    onehot = (cc - c + WIN_W - 1)[None] == np.arange(2 * WIN_W - 1)[:, None, None]
    toep = jnp.einsum('hdk,kxc->hdxc', rpb, jnp.asarray(onehot, F32), precision=lax.Precision.HIGHEST)
    col_start = np.clip(c - WIN_W // 2, 0, GRID_W - WIN_W)
    toep = jnp.where((cc >= col_start) & (cc < col_start + WIN_W), toep, MASKED)
    lo_d = WIN_H - 1 - Q_ROWS
    bias = jnp.concatenate([toep[:, lo_d - a:lo_d - a + n_slot] for a in range(Q_ROWS)], axis=-1)
    full = (n_slot, GRID_W, Q_ROWS, GRID_W)
    j = np.arange(n_slot)[:, None, None, None]
    a = np.arange(Q_ROWS)[None, None, :, None]
    kh = min(WIN_H, rows)
    lows = (np.full_like(a, Q_ROWS), a + Q_ROWS - kh // 2, np.full_like(a, 2 * Q_ROWS - kh))
    row_mask = np.stack([np.where(np.broadcast_to((j >= lo) & (j < lo + kh), full), 0.0, MASKED) for lo in lows])
    return (bias.reshape(n_heads, KEY_SLOTS * Q_BLK, Q_BLK),
            jnp.asarray(row_mask.reshape(3, KEY_SLOTS * Q_BLK, Q_BLK), F32))


MXU_DEPTH = 256
FF_BOUNDS = (0, MXU_DEPTH * (D_FF // MXU_DEPTH + 1) // 2, D_FF)


def _mix_ffn_tail(mix, x_ref, g1_ref, n2_ref, sc2_ref, sh2_ref, g2_ref, wo_ref, win_ref, wout_ref, o_ref):
    tm = x_ref.shape[0]
    n_sub = 2 if tm % (2 * Q_BLK) == 0 else 1
    halves = [slice(i * (tm // n_sub), (i + 1) * (tm // n_sub)) for i in range(n_sub)]

    def head(rows):
        x1 = x_ref[rows, :] + g1_ref[...] * mix(wo_ref, rows)
        return x1, _norm_mod(x1, n2_ref[...], sc2_ref[...], sh2_ref[...]).astype(BF16)

    def ffn(rows, x1, h2):
        acc = None
        for lo, hi in zip(FF_BOUNDS[:-1], FF_BOUNDS[1:]):
            gate = _dot(h2, win_ref[:, lo:hi])
            up = _dot(h2, win_ref[:, D_FF + lo:D_FF + hi])
            t = _dot((_silu(gate) * up).astype(BF16), wout_ref[lo:hi, :])
            acc = t if acc is None else acc + t
        o_ref[rows, :] = x1 + g2_ref[...] * acc

    heads = [head(rows) for rows in halves]
    for rows, (x1, h2) in zip(halves, heads):
        ffn(rows, x1, h2)


def _na_mix_ffn_kernel(a_ref, *rest):
    def mix(wo_ref, rows):
        parts = [_dot(a_ref[g, rows, :], wo_ref[g * GROUP_W:(g + 1) * GROUP_W, :]) for g in range(a_ref.shape[0])]
        return functools.reduce(jnp.add, parts)

    _mix_ffn_tail(mix, *rest)


def _hg_mix_ffn_kernel(of_ref, ob_ref, sg_ref, gn_ref, *rest):
    def mix(wo_ref, rows):
        o = of_ref[rows, :].astype(F32) + ob_ref[rows, :].astype(F32)
        parts = []
        for h in range(HG_HEADS):
            oh = o[:, h * HG_DK:(h + 1) * HG_DK]
            ms = jnp.mean(oh * oh, axis=-1, keepdims=True)
            parts.append(oh * lax.rsqrt(ms + EPS))
        a = (jnp.concatenate(parts, axis=1) * gn_ref[...] * sg_ref[rows, :].astype(F32)).astype(BF16)
        return _dot(a, wo_ref[...])

    _mix_ffn_tail(mix, *rest)


def _mix_ffn(kernel, name, lead, lead_specs, x, g1, n2, sc2, sh2, g2, wo, win, wout, tm):
    B, T, _ = x.shape
    return pl.pallas_call(
        kernel,
        grid=(B, T // tm),
        in_specs=list(lead_specs) + [
            _tok_spec(tm), _mod_spec(g1), _resident((1, D_MODEL)), _mod_spec(sc2), _mod_spec(sh2), _mod_spec(g2),
            _resident(wo.shape), _resident(win.shape), _resident(wout.shape)],
        out_specs=_tok_spec(tm),
        out_shape=jax.ShapeDtypeStruct((B, T, D_MODEL), F32),
        compiler_params=_params("parallel", "parallel"),
        name=name,
    )(*lead, x, g1, n2, sc2, sh2, g2, wo, win, wout)


def _hg_proj_kernel(x_ref, g_ref, sc_ref, sh_ref, w_ref, lb_ref, tri_split_ref, q_ref, v_ref, sg_ref, kk_ref, cum_ref):
    halves = [slice(0, x_ref.shape[0])]
    hs = [_norm_mod(x_ref[rows, :], g_ref[...], sc_ref[...], sh_ref[...]).astype(BF16) for rows in halves]

    def gates(d, rows, f_pre):
        lb = lb_ref[d]
        half = 0.5 * (1.0 - lb)
        th = jnp.tanh(0.5 * f_pre)
        f = jnp.maximum(0.5 * (1.0 + lb) + half * th, GATE_FLOOR)
        kk_ref[d, rows, :] = (half - half * th).astype(BF16)
        hi, lo = _split_bf16(jnp.log2(f))
        for c in range(f_pre.shape[0] // SCAN_CHUNK):
            sub = slice(c * SCAN_CHUNK, (c + 1) * SCAN_CHUNK)
            parts = jnp.concatenate([hi[sub], lo[sub]], axis=0)
            cum_ref[d, rows.start + sub.start:rows.start + sub.stop, :] = _dot(tri_split_ref[d], parts)

    def store(ref, fn, rows, z):
        ref[rows, :] = fn(z).astype(BF16)

    work = []
    for rows, h in zip(halves, hs):
        work += [(h, 3, functools.partial(gates, 0, rows)),
                 (h, 4, functools.partial(gates, 1, rows)),
                 (h, 0, functools.partial(store, q_ref, _silu, rows)),
                 (h, 2, functools.partial(store, sg_ref, _silu, rows)),
                 (h, 1, functools.partial(store, v_ref, lambda z: z, rows))]

    def project(item):
        h, n, _ = item
        return _dot(h, w_ref[:, n * D_MODEL:(n + 1) * D_MODEL])

    pending = project(work[0])
    for n, item in enumerate(work):
        z = pending
        if n + 1 < len(work):
            pending = project(work[n + 1])
        item[2](z)


def _hg_proj(x, g, sc, sh, w, lb, tri_split, tm):
    B, T, _ = x.shape
    tok = jax.ShapeDtypeStruct((B, T, D_MODEL), BF16)
    two = pl.BlockSpec((2, None, tm, D_MODEL), lambda b, i: (0, b, i, 0))
    return pl.pallas_call(
        _hg_proj_kernel,
        grid=(B, T // tm),
        in_specs=[_tok_spec(tm), _resident((1, D_MODEL)), _mod_spec(sc), _mod_spec(sh),
                  _resident(w.shape), _resident(lb.shape), _resident(tri_split.shape)],
        out_specs=[_tok_spec(tm)] * 3 + [two, two],
        out_shape=[tok] * 3 + [jax.ShapeDtypeStruct((2, B, T, D_MODEL), BF16),
                               jax.ShapeDtypeStruct((2, B, T, D_MODEL), F32)],
        compiler_params=_params("parallel", "parallel"),
        name="hg_proj",
    )(x, g, sc, sh, w, lb, tri_split)


def _hg_scan_kernel(qf_ref, vf_ref, kkf_ref, cumf_ref, qb_ref, vb_ref, kkb_ref, cumb_ref, tri_ref, s0_ref,
                    of_ref, ob_ref, st_ref, a_scr):
    @pl.when(pl.program_id(1) == 0)
    def _():
        st_ref[...] = s0_ref[...]

    C = SCAN_CHUNK
    n_sub = qf_ref.shape[0] // C
    dirs = ((qf_ref, vf_ref, kkf_ref, cumf_ref, of_ref), (qb_ref, vb_ref, kkb_ref, cumb_ref, ob_ref))
    streams = [(d, slice((j if d == 0 else n_sub - 1 - j) * C, (j if d == 0 else n_sub - 1 - j) * C + C))
               for j in range(n_sub) for d in range(2)]
    pair_w = 2 * HG_DK
    pairs = [slice(p * pair_w, (p + 1) * pair_w) for p in range(HG_HEADS // 2)]
    first = lax.broadcasted_iota(jnp.int32, (1, pair_w), 1) < HG_DK

    def block_diag(x):
        zero = jnp.zeros_like(x)
        return jnp.concatenate([jnp.where(first, x, zero), jnp.where(first, zero, x)], axis=0)

    def chunk_total(d, rows):
        last = rows.stop - 1 if d == 0 else rows.start
        return dirs[d][3][last:last + 1, :]

    out_of_range = [jnp.max(-0.5 * chunk_total(d, rows)) > FACTOR_RANGE for d, rows in streams]

    def stage_prep(n):
        d, rows = streams[n]
        q_ref, _, kk_ref, cum_ref, _ = dirs[d]
        b = cum_ref[rows, :]
        tot = chunk_total(d, rows)
        mid = 0.5 * tot
        q = q_ref[rows, :]
        kk = kk_ref[rows, :]
        return dict(q_in=q * jnp.exp2(b - mid).astype(BF16), k_in=kk * jnp.exp2(mid - b).astype(BF16),
                    q_st=q * jnp.exp2(b).astype(BF16), k_st=kk * jnp.exp2(tot - b).astype(BF16),
                    decay=jnp.exp2(tot))

    def stage_intra(n, p, check_range):
        d, rows = streams[n]
        seen = jnp.concatenate([tri_ref[d]] * 2, axis=1) > 0.0
        a_s = [jnp.where(seen, _dot_nt(p["q_in"][:, pr], block_diag(p["k_in"][:, pr])), 0.0).astype(BF16)
               for pr in pairs]
        if not check_range:
            return a_s
        a_scr[n] = jnp.concatenate(a_s, axis=1)

        @pl.when(out_of_range[n])
        def _():
            a_scr[n] = _pairwise_intra(dirs[d][0][rows, :].astype(F32), dirs[d][2][rows, :].astype(F32),
                                       dirs[d][3], rows, d == 0).astype(BF16)

        return [a_scr[n, :, i * 2 * C:(i + 1) * 2 * C] for i in range(len(pairs))]

    def stage_values(n, p, a_s):
        d, rows = streams[n]
        v_ref = dirs[d][1]
        o_intra = [_dot(a, block_diag(v_ref[rows, pr])) for a, pr in zip(a_s, pairs)]
        update = [_dot_tn(v_ref[rows, h * HG_DK:(h + 1) * HG_DK], p["k_st"][:, h * HG_DK:(h + 1) * HG_DK])
                  for h in range(HG_HEADS)]
        return o_intra, update

    def stage_state(n, p, o_intra, update):
        d, rows = streams[n]
        zero = jnp.zeros((HG_DK, HG_DK), BF16)
        outs = []
        for i, pr in enumerate(pairs):
            st_a, st_b = st_ref[d, 2 * i], st_ref[d, 2 * i + 1]
            st_bd = jnp.concatenate([jnp.concatenate([st_a.astype(BF16), zero], axis=1),
                                     jnp.concatenate([zero, st_b.astype(BF16)], axis=1)], axis=0)
            outs.append(o_intra[i] + _dot_nt(p["q_st"][:, pr], st_bd))
            st_ref[d, 2 * i] = st_a * p["decay"][:, pr][:, :HG_DK] + update[2 * i]
            st_ref[d, 2 * i + 1] = st_b * p["decay"][:, pr][:, HG_DK:] + update[2 * i + 1]
        dirs[d][4][rows, :] = jnp.concatenate(outs, axis=1).astype(BF16)

    def run(check_range):
        n_str = len(streams)
        prep, intra, vals = {}, {}, {}
        for t in range(n_str + 3):
            if t < n_str:
                prep[t] = stage_prep(t)
            if 0 <= t - 1 < n_str:
                intra[t - 1] = stage_intra(t - 1, prep[t - 1], check_range)
            if 0 <= t - 2 < n_str:
                vals[t - 2] = stage_values(t - 2, prep[t - 2], intra.pop(t - 2))
            if 0 <= t - 3 < n_str:
                stage_state(t - 3, prep.pop(t - 3), *vals.pop(t - 3))

    any_out = functools.reduce(jnp.logical_or, out_of_range)
    pl.when(jnp.logical_not(any_out))(lambda: run(False))
    pl.when(any_out)(lambda: run(True))


def _pairwise_intra(q, kk, cum_ref, rows, forward):
    C = q.shape[0]
    b = cum_ref[rows, :]
    t_idx = lax.broadcasted_iota(jnp.int32, (C, 1), 0)
    lane = lax.broadcasted_iota(jnp.int32, (1, HG_HEADS * C), 1)

    def body(s, acc):
        b_s = cum_ref[pl.ds(rows.start + s, 1), :]
        k_s = jnp.sum(jnp.where(t_idx == s, kk, 0.0), axis=0, keepdims=True)
        w = q * (k_s * jnp.exp2(jnp.minimum(b - b_s, 0.0)))
        seen = (t_idx >= s) if forward else (t_idx <= s)
        for h in range(HG_HEADS):
            col = jnp.sum(w[:, h * HG_DK:(h + 1) * HG_DK], axis=1, keepdims=True)
            acc = jnp.where(lane == h * C + s, jnp.where(seen, col, 0.0), acc)
        return acc

    return lax.fori_loop(0, C, body, jnp.zeros((C, HG_HEADS * C), F32))


def _hg_scan(q, v, kk, cum, tri, s0):
    B, T, _ = q.shape
    C = SCAN_CHUNK * max(s for s in range(1, SCAN_SUB + 1) if T % (SCAN_CHUNK * s) == 0)
    nc = T // C
    fwd = pl.BlockSpec((None, C, D_MODEL), lambda b, i: (b, i, 0))
    bwd = pl.BlockSpec((None, C, D_MODEL), lambda b, i: (b, nc - 1 - i, 0))
    dfwd = pl.BlockSpec((None, None, C, D_MODEL), lambda b, i: (0, b, i, 0))
    dbwd = pl.BlockSpec((None, None, C, D_MODEL), lambda b, i: (1, b, nc - 1 - i, 0))
    state = pl.BlockSpec((2, None, HG_HEADS, HG_DK, HG_DK), lambda b, i: (0, b, 0, 0, 0))
    tok = jax.ShapeDtypeStruct((B, T, D_MODEL), BF16)
    return pl.pallas_call(
        _hg_scan_kernel,
        grid=(B, nc),
        in_specs=[fwd, fwd, dfwd, dfwd, bwd, bwd, dbwd, dbwd, _resident(tri.shape), state],
        out_specs=[fwd, bwd, state],
        out_shape=[tok, tok, jax.ShapeDtypeStruct((2, B, HG_HEADS, HG_DK, HG_DK), F32)],
        scratch_shapes=[pltpu.VMEM((2 * C // SCAN_CHUNK, SCAN_CHUNK, HG_HEADS * SCAN_CHUNK), BF16)],
        compiler_params=_params("parallel", "arbitrary"),
        name="hg_scan",
    )(q, v, kk, cum, q, v, kk, cum, tri, s0)


def _tile(n, pref):
    return pref if n % pref == 0 else n


def kernel(x, c, ctx, c_ctx, ada_w, ada_b, norm1_g, norm2_g, na_w_qkv, na_w_o, na_q_gain, na_k_gain, na_rpb,
           hg_w_in, hg_lower, hg_norm_g, hg_w_o, ffn_w_in, ffn_w_out):
    B, T, _ = x.shape
    L = ctx.shape[1]
    rows = T // GRID_W
    assert B + 1 <= 8 and T % (Q_BLOCKS_PER_STEP * Q_BLK) == 0 and rows >= WIN_H
    assert T % SCAN_CHUNK == 0 and L % Q_BLK == 0
    tm = _tile(T, 512)
    tl = _tile(B * L, 512)
    row = lambda a: a.reshape(1, -1).astype(F32)

    cvec = jnp.zeros((8, D_MODEL), F32).at[:B].set(c).at[B].set(c_ctx)
    mods = _ada(cvec, ada_w, ada_b)

    def mod_rows(i):
        lat = [m.reshape(B, 1, D_MODEL) for m in jnp.split(mods[i, :B], N_MOD, axis=-1)]
        cx = [m.reshape(1, 1, D_MODEL) for m in jnp.split(mods[i, B], N_MOD, axis=-1)]
        return lat, cx

    (sh1, sc1, g1, sh2, sc2, g2), (csh1, csc1, cg1, csh2, csc2, cg2) = mod_rows(0)
    w_qk = na_w_qkv[0, :, :2 * D_MODEL].astype(BF16)
    w_vt = na_w_qkv[0, :, 2 * D_MODEL:].T.astype(BF16)
    gq = row(jnp.tile(na_q_gain[0], NA_HEADS) * (NA_HEAD_DIM ** -0.5 * LOG2E))
    gk = row(jnp.tile(na_k_gain[0], NA_HEADS))
    head_of = np.arange(D_MODEL) // NA_HEAD_DIM
    red = jnp.asarray(head_of[:, None] == np.arange(LANES)[None, :], BF16)
    expand = jnp.asarray(np.arange(2 * LANES)[:, None] % LANES == head_of[None, :], BF16)
    n1 = row(norm1_g[0])
    q_l, k_l, vt_l = _qkv(x, n1, sc1, sh1, w_qk, w_vt, gq, gk, red, expand, tm)
    ctx_flat = ctx.reshape(1, B * L, D_MODEL)
    n_grp = D_MODEL // GROUP_W
    q_c, k_c, vt_c = _qkv(ctx_flat, n1, csc1, csh1, w_qk, w_vt, gq, gk, red, expand, tl)
    q_c, k_c = q_c.reshape(n_grp, B, L, GROUP_W), k_c.reshape(n_grp, B, L, GROUP_W)
    vt_c = vt_c.reshape(B, L // Q_BLK, D_MODEL, Q_BLK)
    rpb2 = na_rpb[0].astype(F32) * LOG2E
    qk_max = NA_HEAD_DIM * jnp.max(jnp.abs(gq)) * jnp.max(jnp.abs(gk))
    bound = qk_max + jnp.maximum(jnp.max(rpb2), 0.0)
    bounded = (bound < BOUNDED_SPREAD).astype(F32).reshape(1)
    bias, row_mask = _attn_bias(rpb2, rows)
    a_l = _attn(bounded, q_l, k_l, vt_l, k_c, vt_c, bias[None] + row_mask[:, None])
    a_c = _attn_ctx(q_c, k_c, vt_c)
    ffn = (na_w_o[0].astype(BF16), ffn_w_in[0].astype(BF16), ffn_w_out[0].astype(BF16))
    n2 = row(norm2_g[0])
    grp_spec = lambda t: pl.BlockSpec((D_MODEL // GROUP_W, None, t, GROUP_W), lambda b, i: (0, b, i, 0))
    x_lat = _mix_ffn(_na_mix_ffn_kernel, "na_mix_ffn", (a_l,), (grp_spec(tm),),
                     x, g1, n2, sc2, sh2, g2, *ffn, tm)
    x_ctx = _mix_ffn(_na_mix_ffn_kernel, "na_mix_ffn_ctx", (a_c.reshape(n_grp, 1, B * L, GROUP_W),), (grp_spec(tl),),
                     ctx_flat, cg1, n2, csc2, csh2, cg2, *ffn, tl)

    (sh1, sc1, g1, sh2, sc2, g2), (csh1, csc1, _, _, _, _) = mod_rows(1)
    lbs = jnp.cumsum(jax.nn.softmax(hg_lower.astype(F32), axis=0), axis=0)
    lb = (lbs - lbs[:1])[1].reshape(2, 1, D_MODEL)
    w_in = hg_w_in[0].astype(BF16)
    n1 = row(norm1_g[1])
    t_idx = np.arange(SCAN_CHUNK)
    tri_np = np.stack([t_idx[None, :] <= t_idx[:, None], t_idx[None, :] >= t_idx[:, None]])
    tri = jnp.asarray(tri_np, F32)
    tri_split = jnp.asarray(np.tile(tri_np, (1, 1, 2)), BF16)
    q_c, v_c, _, kk_c, cum_c = _hg_proj(x_ctx, n1, csc1, csh1, w_in, lb, tri_split, tl)
    q_c, v_c = q_c.reshape(B, L, D_MODEL), v_c.reshape(B, L, D_MODEL)
    kk_c, cum_c = kk_c.reshape(2, B, L, D_MODEL), cum_c.reshape(2, B, L, D_MODEL)
    q_l, v_l, sg_l, kk_l, cum_l = _hg_proj(x_lat, n1, sc1, sh1, w_in, lb, tri_split, tm)
    zero = jnp.zeros((2, B, HG_HEADS, HG_DK, HG_DK), F32)
    _, _, s_ctx = _hg_scan(q_c, v_c, kk_c, cum_c, tri, zero)
    o_f, o_b, _ = _hg_scan(q_l, v_l, kk_l, cum_l, tri, s_ctx)
    ffn = (hg_w_o[0].astype(BF16), ffn_w_in[1].astype(BF16), ffn_w_out[1].astype(BF16))
    gn = row(jnp.tile(hg_norm_g[0], HG_HEADS))
    return _mix_ffn(_hg_mix_ffn_kernel, "hg_mix_ffn", (o_f, o_b, sg_l, gn),
                    (_tok_spec(tm), _tok_spec(tm), _tok_spec(tm), _resident((1, D_MODEL))),
                    x_lat, g1, row(norm2_g[1]), sc2, sh2, g2, *ffn, tm)
```

```python
import functools

import jax
import jax.numpy as jnp
import numpy as np
from jax import lax
from jax.experimental import pallas as pl
from jax.experimental.pallas import tpu as pltpu

F32 = jnp.float32
BF16 = jnp.bfloat16

D_MODEL = 1024
EPS = 1e-6
GRID_W = 64
WIN_H = 8
WIN_W = 16
NA_HEADS = 16
NA_HEAD_DIM = D_MODEL // NA_HEADS
HG_HEADS = 8
HG_DK = 128
D_FF = 2816
N_MOD = 6

LANES = 128
Q_ROWS = 4
Q_BLK = Q_ROWS * GRID_W
KEY_SLOTS = 3
ATTN_GROUP = 4
GROUP_W = ATTN_GROUP * NA_HEAD_DIM
BOUNDED_SPREAD = 60.0
Q_BLOCKS_PER_STEP = 8
LOG2E = 1.4426950408889634
MASKED = -1e30
SCAN_CHUNK = 64
SCAN_SUB = 8
FACTOR_RANGE = 100.0
GATE_FLOOR = 2.0 ** -100
VMEM_LIMIT = 56 * 1024 * 1024


def _dot(a, b):
    return jnp.dot(a, b, preferred_element_type=F32)


def _dot_nt(a, b):
    return lax.dot_general(a, b, (((1,), (1,)), ((), ())), preferred_element_type=F32)


def _dot_tn(a, b):
    return lax.dot_general(a, b, (((0,), (0,)), ((), ())), preferred_element_type=F32)


def _silu(x):
    half = 0.5 * x
    return half + half * jnp.tanh(half)


def _norm_mod(xf, g, sc, sh):
    ms = jnp.mean(xf * xf, axis=-1, keepdims=True)
    return (xf * lax.rsqrt(ms + EPS) * g) * (1.0 + sc) + sh


def _split_bf16(x):
    hi = x.astype(BF16)
    lo = (x - hi.astype(F32)).astype(BF16)
    return hi, lo


def _resident(shape):
    zeros = (0,) * len(shape)
    return pl.BlockSpec(shape, lambda *_: zeros, pipeline_mode=pl.Buffered(1))


def _mod_spec(arr):
    if arr.shape[0] == 1:
        return pl.BlockSpec((None, 1, D_MODEL), lambda b, i: (0, 0, 0))
    return pl.BlockSpec((None, 1, D_MODEL), lambda b, i: (b, 0, 0))


def _tok_spec(tm, width=D_MODEL):
    return pl.BlockSpec((None, tm, width), lambda b, i: (b, i, 0))


def _params(*sem):
    return pltpu.CompilerParams(dimension_semantics=sem, vmem_limit_bytes=VMEM_LIMIT)


def _ada_kernel(c_ref, w_ref, b_ref, o_ref):
    s = _silu(c_ref[...]).astype(BF16)
    o_ref[...] = _dot(s, w_ref[...].astype(BF16)) + b_ref[...]


def _ada(cvec, ada_w, ada_b):
    depth, _, n = ada_w.shape
    tn = 1536
    return pl.pallas_call(
        _ada_kernel,
        grid=(depth, n // tn),
        in_specs=[pl.BlockSpec((8, D_MODEL), lambda l, j: (0, 0)),
                  pl.BlockSpec((None, D_MODEL, tn), lambda l, j: (l, 0, j)),
                  pl.BlockSpec((None, 1, tn), lambda l, j: (l, 0, j))],
        out_specs=pl.BlockSpec((None, 8, tn), lambda l, j: (l, 0, j)),
        out_shape=jax.ShapeDtypeStruct((depth, 8, n), F32),
        compiler_params=_params("arbitrary", "arbitrary"),
        name="ada",
    )(cvec, ada_w, ada_b.reshape(depth, 1, n))


def _qkv_kernel(x_ref, g_ref, sc_ref, sh_ref, w_ref, wvt_ref, gq_ref, gk_ref, red_ref, exp_ref,
                q_ref, k_ref, vt_ref):
    h = _norm_mod(x_ref[...], g_ref[...], sc_ref[...], sh_ref[...]).astype(BF16)

    def head_norm(z, gain, out_ref):
        ss = _dot((z * z).astype(BF16), red_ref[...])
        r = lax.rsqrt(ss * (1.0 / NA_HEAD_DIM) + EPS)
        rex = _dot(jnp.concatenate(_split_bf16(r), axis=1), exp_ref[...])
        zn = (z * rex * gain).astype(BF16)
        for g in range(D_MODEL // GROUP_W):
            out_ref[g] = zn[:, g * GROUP_W:(g + 1) * GROUP_W]

    q = _dot(h, w_ref[:, 0:D_MODEL])
    k = _dot(h, w_ref[:, D_MODEL:2 * D_MODEL])
    head_norm(q, gq_ref[...], q_ref)
    vt = _dot_nt(wvt_ref[...], h).astype(BF16)
    head_norm(k, gk_ref[...], k_ref)
    for j in range(vt_ref.shape[0]):
        vt_ref[j] = vt[:, j * Q_BLK:(j + 1) * Q_BLK]


def _qkv(x, g, sc, sh, w_qk, w_vt, gq, gk, red, expand, tm):
    B, T, _ = x.shape
    n_grp = D_MODEL // GROUP_W
    grp = jax.ShapeDtypeStruct((n_grp, B, T, GROUP_W), BF16)
    grp_spec = pl.BlockSpec((n_grp, None, tm, GROUP_W), lambda b, i: (0, b, i, 0))
    return pl.pallas_call(
        _qkv_kernel,
        grid=(B, T // tm),
        in_specs=[_tok_spec(tm), _resident((1, D_MODEL)), _mod_spec(sc), _mod_spec(sh),
                  _resident(w_qk.shape), _resident(w_vt.shape), _resident((1, D_MODEL)), _resident((1, D_MODEL)),
                  _resident(red.shape), _resident(expand.shape)],
        out_specs=[grp_spec, grp_spec,
                   pl.BlockSpec((None, tm // Q_BLK, D_MODEL, Q_BLK), lambda b, i: (b, i, 0, 0))],
        out_shape=[grp, grp, jax.ShapeDtypeStruct((B, T // Q_BLK, D_MODEL, Q_BLK), BF16)],
        compiler_params=_params("parallel", "parallel"),
        name="qkv",
    )(x, g, sc, sh, w_qk, w_vt, gq, gk, red, expand)


def _attend(tasks):
    def stage_scores(task):
        s = _scores(task)
        m = functools.reduce(jnp.maximum, [jnp.max(t, axis=0, keepdims=True) for t in s])
        return s, m

    def stage_probs(sm):
        s, m = sm
        return [jnp.exp2(t - m).astype(BF16) for t in s]

    n = len(tasks)
    scores, probs, outs = {}, {}, []
    for step in range(n + 2):
        if step < n:
            scores[step] = stage_scores(tasks[step])
        if 0 <= step - 1 < n:
            probs[step - 1] = stage_probs(scores.pop(step - 1))
        if 0 <= step - 2 < n:
            outs.append(_weighted_values(tasks[step - 2], probs.pop(step - 2)))
    return outs


def _scores(task):
    q_h, keys, _, biases = task
    s = []
    for key, bias in zip(keys, biases):
        t = _dot_nt(key(), q_h)
        s.append(t if bias is None else t + bias())
    return s


def _attend_bounded(tasks):
    def stage_probs(task):
        return [jnp.exp2(t).astype(BF16) for t in _scores(task)]

    n = len(tasks)
    probs, outs = {}, []
    for step in range(n + 1):
        if step < n:
            probs[step] = stage_probs(tasks[step])
        if step >= 1:
            outs.append(_weighted_values(tasks[step - 1], probs.pop(step - 1)))
    return outs


def _weighted_values(task, p):
    ones_rows = 16
    acc = None
    for vt, t in zip(task[2], p):
        v1 = jnp.concatenate([vt(), jnp.ones((ones_rows, t.shape[0]), BF16)], axis=0)
        part = _dot(v1, t)
        acc = part if acc is None else acc + part
    return acc[:NA_HEAD_DIM] * (1.0 / acc[NA_HEAD_DIM:NA_HEAD_DIM + 1])


def _head_queries(q_grp):
    lane_head = lax.broadcasted_iota(jnp.int32, (1, q_grp.shape[1]), 1) // NA_HEAD_DIM
    return [jnp.where(lane_head == h, q_grp, jnp.zeros_like(q_grp)) for h in range(ATTN_GROUP)]


def _head_rows(ref, h):
    return lambda: ref[h * NA_HEAD_DIM:(h + 1) * NA_HEAD_DIM, :]


def _attn_kernel(bounded_ref, q_ref, *refs):
    n_halo = Q_BLOCKS_PER_STEP + KEY_SLOTS - 1
    k_refs, v_refs = refs[:n_halo], refs[n_halo:2 * n_halo]
    kc_ref, vc_ref = refs[2 * n_halo:2 * n_halo + 2]
    bias_first, bias_mid, bias_last, o_ref = refs[2 * n_halo + 2:]
    bias_refs = [bias_first] + [bias_mid] * (Q_BLOCKS_PER_STEP - 2) + [bias_last]
    n_ctx = kc_ref.shape[0] // Q_BLK
    tasks = []
    for jb in range(Q_BLOCKS_PER_STEP):
        q_heads = _head_queries(q_ref[jb * Q_BLK:(jb + 1) * Q_BLK, :])
        for h in range(ATTN_GROUP):
            keys = ([(lambda r=r: r[...]) for r in k_refs[jb:jb + KEY_SLOTS]]
                    + [(lambda t=t: kc_ref[t * Q_BLK:(t + 1) * Q_BLK, :]) for t in range(n_ctx)])
            vts = ([_head_rows(r, h) for r in v_refs[jb:jb + KEY_SLOTS]]
                   + [_head_rows(vc_ref.at[t], h) for t in range(n_ctx)])
            biases = ([(lambda t=t, jb=jb, h=h: bias_refs[jb][h, t * Q_BLK:(t + 1) * Q_BLK, :])
                       for t in range(KEY_SLOTS)] + [None] * n_ctx)
            tasks.append((q_heads[h], keys, vts, biases))

    def finish(outs):
        for jb in range(Q_BLOCKS_PER_STEP):
            o_t = jnp.concatenate(outs[jb * ATTN_GROUP:(jb + 1) * ATTN_GROUP], axis=0)
            o_ref[jb * Q_BLK:(jb + 1) * Q_BLK, :] = o_t.T.astype(BF16)

    bounded = bounded_ref[0] > 0.5

    @pl.when(bounded)
    def _():
        finish(_attend_bounded(tasks))

    @pl.when(jnp.logical_not(bounded))
    def _():
        finish(_attend(tasks))


def _attn(bounded, q, k, vt, kc, vtc, bias):
    n_grp, B, T, gw = q.shape
    L = kc.shape[2]
    nb = T // Q_BLK
    per = Q_BLOCKS_PER_STEP
    n_halo = per + KEY_SLOTS - 1

    def blk(i, d):
        return jnp.clip(per * i + d, 0, nb - 1)

    def variant(qb):
        return jnp.where(qb == 0, 0, jnp.where(qb == nb - 1, 2, 1))

    halo = [pl.BlockSpec((None, None, Q_BLK, gw), lambda g, b, i, d=d: (g, b, blk(i, d - 1), 0))
            for d in range(n_halo)]
    halo_t = [pl.BlockSpec((None, None, gw, Q_BLK), lambda g, b, i, d=d: (b, blk(i, d - 1), g, 0))
              for d in range(n_halo)]
    biases = [pl.BlockSpec((None, ATTN_GROUP, KEY_SLOTS * Q_BLK, Q_BLK), index)
              for index in (lambda g, b, i: (variant(per * i), g, 0, 0),
                            lambda g, b, i: (1, g, 0, 0),
                            lambda g, b, i: (variant(per * i + per - 1), g, 0, 0))]
    tok = pl.BlockSpec((None, None, per * Q_BLK, gw), lambda g, b, i: (g, b, i, 0))
    return pl.pallas_call(
        _attn_kernel,
        grid=(n_grp, B, nb // per),
        in_specs=[pl.BlockSpec(memory_space=pltpu.SMEM), tok] + halo + halo_t
                 + [pl.BlockSpec((None, None, L, gw), lambda g, b, i: (g, b, 0, 0)),
                    pl.BlockSpec((None, L // Q_BLK, gw, Q_BLK), lambda g, b, i: (b, 0, g, 0))] + biases,
        out_specs=tok,
        out_shape=jax.ShapeDtypeStruct((n_grp, B, T, gw), BF16),
        compiler_params=_params("parallel", "parallel", "parallel"),
        name="attn",
    )(bounded, q, *([k] * n_halo), *([vt] * n_halo), kc, vtc, bias, bias, bias)


def _attn_ctx_kernel(q_ref, k_ref, vt_ref, o_ref):
    n_ctx = k_ref.shape[1] // Q_BLK
    tasks = []
    for g in range(q_ref.shape[0]):
        q_heads = _head_queries(q_ref[g])
        for h in range(ATTN_GROUP):
            rows = slice(g * GROUP_W + h * NA_HEAD_DIM, g * GROUP_W + (h + 1) * NA_HEAD_DIM)
            tasks.append((q_heads[h],
                          [(lambda t=t, g=g: k_ref[g, t * Q_BLK:(t + 1) * Q_BLK, :]) for t in range(n_ctx)],
                          [(lambda t=t, rows=rows: vt_ref[t, rows, :]) for t in range(n_ctx)],
                          [None] * n_ctx))
    outs = _attend(tasks)
    for g in range(q_ref.shape[0]):
        o_ref[g] = jnp.concatenate(outs[g * ATTN_GROUP:(g + 1) * ATTN_GROUP], axis=0).T.astype(BF16)


def _attn_ctx(q, k, vt):
    n_grp, B, L, gw = q.shape
    spec = pl.BlockSpec((n_grp, None, L, gw), lambda b: (0, b, 0, 0))
    return pl.pallas_call(
        _attn_ctx_kernel,
        grid=(B,),
        in_specs=[spec, spec, pl.BlockSpec((None, L // Q_BLK, D_MODEL, Q_BLK), lambda b: (b, 0, 0, 0))],
        out_specs=spec,
        out_shape=jax.ShapeDtypeStruct((n_grp, B, L, gw), BF16),
        compiler_params=_params("parallel"),
        name="attn_ctx",
    )(q, k, vt)


def _attn_bias(rpb, rows):
    n_heads = rpb.shape[0]
    n_slot = KEY_SLOTS * Q_ROWS
    cc = np.arange(GRID_W)[:, None]
    c = np.arange(GRID_W)[None, :]
    onehot = (cc - c + WIN_W - 1)[None] == np.arange(2 * WIN_W - 1)[:, None, None]
    toep = jnp.einsum('hdk,kxc->hdxc', rpb, jnp.asarray(onehot, F32), precision=lax.Precision.HIGHEST)
    col_start = np.clip(c - WIN_W // 2, 0, GRID_W - WIN_W)
    toep = jnp.where((cc >= col_start) & (cc < col_start + WIN_W), toep, MASKED)
    lo_d = WIN_H - 1 - Q_ROWS
    bias = jnp.concatenate([toep[:, lo_d - a:lo_d - a + n_slot] for a in range(Q_ROWS)], axis=-1)
    full = (n_slot, GRID_W, Q_ROWS, GRID_W)
    j = np.arange(n_slot)[:, None, None, None]
    a = np.arange(Q_ROWS)[None, None, :, None]
    kh = min(WIN_H, rows)
    lows = (np.full_like(a, Q_ROWS), a + Q_ROWS - kh // 2, np.full_like(a, 2 * Q_ROWS - kh))
    row_mask = np.stack([np.where(np.broadcast_to((j >= lo) & (j < lo + kh), full), 0.0, MASKED) for lo in lows])
    return (bias.reshape(n_heads, KEY_SLOTS * Q_BLK, Q_BLK),
            jnp.asarray(row_mask.reshape(3, KEY_SLOTS * Q_BLK, Q_BLK), F32))


MXU_DEPTH = 256
FF_BOUNDS = (0, MXU_DEPTH * (D_FF // MXU_DEPTH + 1) // 2, D_FF)


def _mix_ffn_tail(mix, x_ref, g1_ref, n2_ref, sc2_ref, sh2_ref, g2_ref, wo_ref, win_ref, wout_ref, o_ref):
    tm = x_ref.shape[0]
    n_sub = 2 if tm % (2 * Q_BLK) == 0 else 1
    halves = [slice(i * (tm // n_sub), (i + 1) * (tm // n_sub)) for i in range(n_sub)]

    def head(rows):
        x1 = x_ref[rows, :] + g1_ref[...] * mix(wo_ref, rows)
        return x1, _norm_mod(x1, n2_ref[...], sc2_ref[...], sh2_ref[...]).astype(BF16)

    def ffn(rows, x1, h2):
        acc = None
        for lo, hi in zip(FF_BOUNDS[:-1], FF_BOUNDS[1:]):
            gate = _dot(h2, win_ref[:, lo:hi])
            up = _dot(h2, win_ref[:, D_FF + lo:D_FF + hi])
            t = _dot((_silu(gate) * up).astype(BF16), wout_ref[lo:hi, :])
            acc = t if acc is None else acc + t
        o_ref[rows, :] = x1 + g2_ref[...] * acc

    heads = [head(rows) for rows in halves]
    for rows, (x1, h2) in zip(halves, heads):
        ffn(rows, x1, h2)


def _na_mix_ffn_kernel(a_ref, *rest):
    def mix(wo_ref, rows):
        parts = [_dot(a_ref[g, rows, :], wo_ref[g * GROUP_W:(g + 1) * GROUP_W, :]) for g in range(a_ref.shape[0])]
        return functools.reduce(jnp.add, parts)

    _mix_ffn_tail(mix, *rest)


def _hg_mix_ffn_kernel(of_ref, ob_ref, sg_ref, gn_ref, *rest):
    def mix(wo_ref, rows):
        o = of_ref[rows, :].astype(F32) + ob_ref[rows, :].astype(F32)
        parts = []
        for h in range(HG_HEADS):
            oh = o[:, h * HG_DK:(h + 1) * HG_DK]
            ms = jnp.mean(oh * oh, axis=-1, keepdims=True)
            parts.append(oh * lax.rsqrt(ms + EPS))
        a = (jnp.concatenate(parts, axis=1) * gn_ref[...] * sg_ref[rows, :].astype(F32)).astype(BF16)
        return _dot(a, wo_ref[...])

    _mix_ffn_tail(mix, *rest)


def _mix_ffn(kernel, name, lead, lead_specs, x, g1, n2, sc2, sh2, g2, wo, win, wout, tm):
    B, T, _ = x.shape
    return pl.pallas_call(
        kernel,
        grid=(B, T // tm),
        in_specs=list(lead_specs) + [
            _tok_spec(tm), _mod_spec(g1), _resident((1, D_MODEL)), _mod_spec(sc2), _mod_spec(sh2), _mod_spec(g2),
            _resident(wo.shape), _resident(win.shape), _resident(wout.shape)],
        out_specs=_tok_spec(tm),
        out_shape=jax.ShapeDtypeStruct((B, T, D_MODEL), F32),
        compiler_params=_params("parallel", "parallel"),
        name=name,
    )(*lead, x, g1, n2, sc2, sh2, g2, wo, win, wout)


def _hg_proj_kernel(x_ref, g_ref, sc_ref, sh_ref, w_ref, lb_ref, tri_split_ref, q_ref, v_ref, sg_ref, kk_ref, cum_ref):
    halves = [slice(0, x_ref.shape[0])]
    hs = [_norm_mod(x_ref[rows, :], g_ref[...], sc_ref[...], sh_ref[...]).astype(BF16) for rows in halves]

    def gates(d, rows, f_pre):
        lb = lb_ref[d]
        half = 0.5 * (1.0 - lb)
        th = jnp.tanh(0.5 * f_pre)
        f = jnp.maximum(0.5 * (1.0 + lb) + half * th, GATE_FLOOR)
        kk_ref[d, rows, :] = (half - half * th).astype(BF16)
        hi, lo = _split_bf16(jnp.log2(f))
        for c in range(f_pre.shape[0] // SCAN_CHUNK):
            sub = slice(c * SCAN_CHUNK, (c + 1) * SCAN_CHUNK)
            parts = jnp.concatenate([hi[sub], lo[sub]], axis=0)
            cum_ref[d, rows.start + sub.start:rows.start + sub.stop, :] = _dot(tri_split_ref[d], parts)

    def store(ref, fn, rows, z):
        ref[rows, :] = fn(z).astype(BF16)

    work = []
    for rows, h in zip(halves, hs):
        work += [(h, 3, functools.partial(gates, 0, rows)),
                 (h, 4, functools.partial(gates, 1, rows)),
                 (h, 0, functools.partial(store, q_ref, _silu, rows)),
                 (h, 2, functools.partial(store, sg_ref, _silu, rows)),
                 (h, 1, functools.partial(store, v_ref, lambda z: z, rows))]

    def project(item):
        h, n, _ = item
        return _dot(h, w_ref[:, n * D_MODEL:(n + 1) * D_MODEL])

    pending = project(work[0])
    for n, item in enumerate(work):
        z = pending
        if n + 1 < len(work):
            pending = project(work[n + 1])
        item[2](z)


def _hg_proj(x, g, sc, sh, w, lb, tri_split, tm):
    B, T, _ = x.shape
    tok = jax.ShapeDtypeStruct((B, T, D_MODEL), BF16)
    two = pl.BlockSpec((2, None, tm, D_MODEL), lambda b, i: (0, b, i, 0))
    return pl.pallas_call(
        _hg_proj_kernel,
        grid=(B, T // tm),
        in_specs=[_tok_spec(tm), _resident((1, D_MODEL)), _mod_spec(sc), _mod_spec(sh),
                  _resident(w.shape), _resident(lb.shape), _resident(tri_split.shape)],
        out_specs=[_tok_spec(tm)] * 3 + [two, two],
        out_shape=[tok] * 3 + [jax.ShapeDtypeStruct((2, B, T, D_MODEL), BF16),
                               jax.ShapeDtypeStruct((2, B, T, D_MODEL), F32)],
        compiler_params=_params("parallel", "parallel"),
        name="hg_proj",
    )(x, g, sc, sh, w, lb, tri_split)


def _hg_scan_kernel(qf_ref, vf_ref, kkf_ref, cumf_ref, qb_ref, vb_ref, kkb_ref, cumb_ref, tri_ref, s0_ref,
                    of_ref, ob_ref, st_ref, a_scr):
    @pl.when(pl.program_id(1) == 0)
    def _():
        st_ref[...] = s0_ref[...]

    C = SCAN_CHUNK
    n_sub = qf_ref.shape[0] // C
    dirs = ((qf_ref, vf_ref, kkf_ref, cumf_ref, of_ref), (qb_ref, vb_ref, kkb_ref, cumb_ref, ob_ref))
    streams = [(d, slice((j if d == 0 else n_sub - 1 - j) * C, (j if d == 0 else n_sub - 1 - j) * C + C))
               for j in range(n_sub) for d in range(2)]
    pair_w = 2 * HG_DK
    pairs = [slice(p * pair_w, (p + 1) * pair_w) for p in range(HG_HEADS // 2)]
    first = lax.broadcasted_iota(jnp.int32, (1, pair_w), 1) < HG_DK

    def block_diag(x):
        zero = jnp.zeros_like(x)
        return jnp.concatenate([jnp.where(first, x, zero), jnp.where(first, zero, x)], axis=0)

    def chunk_total(d, rows):
        last = rows.stop - 1 if d == 0 else rows.start
        return dirs[d][3][last:last + 1, :]

    out_of_range = [jnp.max(-0.5 * chunk_total(d, rows)) > FACTOR_RANGE for d, rows in streams]

    def stage_prep(n):
        d, rows = streams[n]
        q_ref, _, kk_ref, cum_ref, _ = dirs[d]
        b = cum_ref[rows, :]
        tot = chunk_total(d, rows)
        mid = 0.5 * tot
        q = q_ref[rows, :]
        kk = kk_ref[rows, :]
        return dict(q_in=q * jnp.exp2(b - mid).astype(BF16), k_in=kk * jnp.exp2(mid - b).astype(BF16),
                    q_st=q * jnp.exp2(b).astype(BF16), k_st=kk * jnp.exp2(tot - b).astype(BF16),
                    decay=jnp.exp2(tot))

    def stage_intra(n, p, check_range):
        d, rows = streams[n]
        seen = jnp.concatenate([tri_ref[d]] * 2, axis=1) > 0.0
        a_s = [jnp.where(seen, _dot_nt(p["q_in"][:, pr], block_diag(p["k_in"][:, pr])), 0.0).astype(BF16)
               for pr in pairs]
        if not check_range:
            return a_s
        a_scr[n] = jnp.concatenate(a_s, axis=1)

        @pl.when(out_of_range[n])
        def _():
            a_scr[n] = _pairwise_intra(dirs[d][0][rows, :].astype(F32), dirs[d][2][rows, :].astype(F32),
                                       dirs[d][3], rows, d == 0).astype(BF16)

        return [a_scr[n, :, i * 2 * C:(i + 1) * 2 * C] for i in range(len(pairs))]

    def stage_values(n, p, a_s):
        d, rows = streams[n]
        v_ref = dirs[d][1]
        o_intra = [_dot(a, block_diag(v_ref[rows, pr])) for a, pr in zip(a_s, pairs)]
        update = [_dot_tn(v_ref[rows, h * HG_DK:(h + 1) * HG_DK], p["k_st"][:, h * HG_DK:(h + 1) * HG_DK])
                  for h in range(HG_HEADS)]
        return o_intra, update

    def stage_state(n, p, o_intra, update):
        d, rows = streams[n]
        zero = jnp.zeros((HG_DK, HG_DK), BF16)
        outs = []
        for i, pr in enumerate(pairs):
            st_a, st_b = st_ref[d, 2 * i], st_ref[d, 2 * i + 1]
            st_bd = jnp.concatenate([jnp.concatenate([st_a.astype(BF16), zero], axis=1),
                                     jnp.concatenate([zero, st_b.astype(BF16)], axis=1)], axis=0)
            outs.append(o_intra[i] + _dot_nt(p["q_st"][:, pr], st_bd))
            st_ref[d, 2 * i] = st_a * p["decay"][:, pr][:, :HG_DK] + update[2 * i]
            st_ref[d, 2 * i + 1] = st_b * p["decay"][:, pr][:, HG_DK:] + update[2 * i + 1]
        dirs[d][4][rows, :] = jnp.concatenate(outs, axis=1).astype(BF16)

    def run(check_range):
        n_str = len(streams)
        prep, intra, vals = {}, {}, {}
        for t in range(n_str + 3):
            if t < n_str:
                prep[t] = stage_prep(t)
            if 0 <= t - 1 < n_str:
                intra[t - 1] = stage_intra(t - 1, prep[t - 1], check_range)
            if 0 <= t - 2 < n_str:
                vals[t - 2] = stage_values(t - 2, prep[t - 2], intra.pop(t - 2))
            if 0 <= t - 3 < n_str:
                stage_state(t - 3, prep.pop(t - 3), *vals.pop(t - 3))

    any_out = functools.reduce(jnp.logical_or, out_of_range)
    pl.when(jnp.logical_not(any_out))(lambda: run(False))
    pl.when(any_out)(lambda: run(True))


def _pairwise_intra(q, kk, cum_ref, rows, forward):
    C = q.shape[0]
    b = cum_ref[rows, :]
    t_idx = lax.broadcasted_iota(jnp.int32, (C, 1), 0)
    lane = lax.broadcasted_iota(jnp.int32, (1, HG_HEADS * C), 1)

    def body(s, acc):
        b_s = cum_ref[pl.ds(rows.start + s, 1), :]
        k_s = jnp.sum(jnp.where(t_idx == s, kk, 0.0), axis=0, keepdims=True)
        w = q * (k_s * jnp.exp2(jnp.minimum(b - b_s, 0.0)))
        seen = (t_idx >= s) if forward else (t_idx <= s)
        for h in range(HG_HEADS):
            col = jnp.sum(w[:, h * HG_DK:(h + 1) * HG_DK], axis=1, keepdims=True)
            acc = jnp.where(lane == h * C + s, jnp.where(seen, col, 0.0), acc)
        return acc

    return lax.fori_loop(0, C, body, jnp.zeros((C, HG_HEADS * C), F32))


def _hg_scan(q, v, kk, cum, tri, s0):
    B, T, _ = q.shape
    C = SCAN_CHUNK * max(s for s in range(1, SCAN_SUB + 1) if T % (SCAN_CHUNK * s) == 0)
    nc = T // C
    fwd = pl.BlockSpec((None, C, D_MODEL), lambda b, i: (b, i, 0))
    bwd = pl.BlockSpec((None, C, D_MODEL), lambda b, i: (b, nc - 1 - i, 0))
    dfwd = pl.BlockSpec((None, None, C, D_MODEL), lambda b, i: (0, b, i, 0))
    dbwd = pl.BlockSpec((None, None, C, D_MODEL), lambda b, i: (1, b, nc - 1 - i, 0))
    state = pl.BlockSpec((2, None, HG_HEADS, HG_DK, HG_DK), lambda b, i: (0, b, 0, 0, 0))
    tok = jax.ShapeDtypeStruct((B, T, D_MODEL), BF16)
    return pl.pallas_call(
        _hg_scan_kernel,
        grid=(B, nc),
        in_specs=[fwd, fwd, dfwd, dfwd, bwd, bwd, dbwd, dbwd, _resident(tri.shape), state],
        out_specs=[fwd, bwd, state],
        out_shape=[tok, tok, jax.ShapeDtypeStruct((2, B, HG_HEADS, HG_DK, HG_DK), F32)],
        scratch_shapes=[pltpu.VMEM((2 * C // SCAN_CHUNK, SCAN_CHUNK, HG_HEADS * SCAN_CHUNK), BF16)],
        compiler_params=_params("parallel", "arbitrary"),
        name="hg_scan",
    )(q, v, kk, cum, q, v, kk, cum, tri, s0)


def _tile(n, pref):
    return pref if n % pref == 0 else n


def kernel(x, c, ctx, c_ctx, ada_w, ada_b, norm1_g, norm2_g, na_w_qkv, na_w_o, na_q_gain, na_k_gain, na_rpb,
           hg_w_in, hg_lower, hg_norm_g, hg_w_o, ffn_w_in, ffn_w_out):
    B, T, _ = x.shape
    L = ctx.shape[1]
    rows = T // GRID_W
    assert B + 1 <= 8 and T % (Q_BLOCKS_PER_STEP * Q_BLK) == 0 and rows >= WIN_H
    assert T % SCAN_CHUNK == 0 and L % Q_BLK == 0
    tm = _tile(T, 512)
    tl = _tile(B * L, 512)
    row = lambda a: a.reshape(1, -1).astype(F32)

    cvec = jnp.zeros((8, D_MODEL), F32).at[:B].set(c).at[B].set(c_ctx)
    mods = _ada(cvec, ada_w, ada_b)

    def mod_rows(i):
        lat = [m.reshape(B, 1, D_MODEL) for m in jnp.split(mods[i, :B], N_MOD, axis=-1)]
        cx = [m.reshape(1, 1, D_MODEL) for m in jnp.split(mods[i, B], N_MOD, axis=-1)]
        return lat, cx

    (sh1, sc1, g1, sh2, sc2, g2), (csh1, csc1, cg1, csh2, csc2, cg2) = mod_rows(0)
    w_qk = na_w_qkv[0, :, :2 * D_MODEL].astype(BF16)
    w_vt = na_w_qkv[0, :, 2 * D_MODEL:].T.astype(BF16)
    gq = row(jnp.tile(na_q_gain[0], NA_HEADS) * (NA_HEAD_DIM ** -0.5 * LOG2E))
    gk = row(jnp.tile(na_k_gain[0], NA_HEADS))
    head_of = np.arange(D_MODEL) // NA_HEAD_DIM
    red = jnp.asarray(head_of[:, None] == np.arange(LANES)[None, :], BF16)
    expand = jnp.asarray(np.arange(2 * LANES)[:, None] % LANES == head_of[None, :], BF16)
    n1 = row(norm1_g[0])
    q_l, k_l, vt_l = _qkv(x, n1, sc1, sh1, w_qk, w_vt, gq, gk, red, expand, _tile(T, 1024))
    ctx_flat = ctx.reshape(1, B * L, D_MODEL)
    n_grp = D_MODEL // GROUP_W
    q_c, k_c, vt_c = _qkv(ctx_flat, n1, csc1, csh1, w_qk, w_vt, gq, gk, red, expand, tl)
    q_c, k_c = q_c.reshape(n_grp, B, L, GROUP_W), k_c.reshape(n_grp, B, L, GROUP_W)
    vt_c = vt_c.reshape(B, L // Q_BLK, D_MODEL, Q_BLK)
    rpb2 = na_rpb[0].astype(F32) * LOG2E
    qk_max = NA_HEAD_DIM * jnp.max(jnp.abs(gq)) * jnp.max(jnp.abs(gk))
    bound = qk_max + jnp.maximum(jnp.max(rpb2), 0.0)
    bounded = (bound < BOUNDED_SPREAD).astype(F32).reshape(1)
    bias, row_mask = _attn_bias(rpb2, rows)
    a_l = _attn(bounded, q_l, k_l, vt_l, k_c, vt_c, bias[None] + row_mask[:, None])
    a_c = _attn_ctx(q_c, k_c, vt_c)
    ffn = (na_w_o[0].astype(BF16), ffn_w_in[0].astype(BF16), ffn_w_out[0].astype(BF16))
    n2 = row(norm2_g[0])
    grp_spec = lambda t: pl.BlockSpec((D_MODEL // GROUP_W, None, t, GROUP_W), lambda b, i: (0, b, i, 0))
    x_lat = _mix_ffn(_na_mix_ffn_kernel, "na_mix_ffn", (a_l,), (grp_spec(tm),),
                     x, g1, n2, sc2, sh2, g2, *ffn, tm)
    x_ctx = _mix_ffn(_na_mix_ffn_kernel, "na_mix_ffn_ctx", (a_c.reshape(n_grp, 1, B * L, GROUP_W),), (grp_spec(tl),),
                     ctx_flat, cg1, n2, csc2, csh2, cg2, *ffn, tl)

    (sh1, sc1, g1, sh2, sc2, g2), (csh1, csc1, _, _, _, _) = mod_rows(1)
    lbs = jnp.cumsum(jax.nn.softmax(hg_lower.astype(F32), axis=0), axis=0)
    lb = (lbs - lbs[:1])[1].reshape(2, 1, D_MODEL)
    w_in = hg_w_in[0].astype(BF16)
    n1 = row(norm1_g[1])
    t_idx = np.arange(SCAN_CHUNK)
    tri_np = np.stack([t_idx[None, :] <= t_idx[:, None], t_idx[None, :] >= t_idx[:, None]])
    tri = jnp.asarray(tri_np, F32)
    tri_split = jnp.asarray(np.tile(tri_np, (1, 1, 2)), BF16)
    q_c, v_c, _, kk_c, cum_c = _hg_proj(x_ctx, n1, csc1, csh1, w_in, lb, tri_split, tl)
    q_c, v_c = q_c.reshape(B, L, D_MODEL), v_c.reshape(B, L, D_MODEL)
    kk_c, cum_c = kk_c.reshape(2, B, L, D_MODEL), cum_c.reshape(2, B, L, D_MODEL)
    q_l, v_l, sg_l, kk_l, cum_l = _hg_proj(x_lat, n1, sc1, sh1, w_in, lb, tri_split, tm)
    zero = jnp.zeros((2, B, HG_HEADS, HG_DK, HG_DK), F32)
    _, _, s_ctx = _hg_scan(q_c, v_c, kk_c, cum_c, tri, zero)
    o_f, o_b, _ = _hg_scan(q_l, v_l, kk_l, cum_l, tri, s_ctx)
    ffn = (hg_w_o[0].astype(BF16), ffn_w_in[1].astype(BF16), ffn_w_out[1].astype(BF16))
    gn = row(jnp.tile(hg_norm_g[0], HG_HEADS))
    return _mix_ffn(_hg_mix_ffn_kernel, "hg_mix_ffn", (o_f, o_b, sg_l, gn),
                    (_tok_spec(tm), _tok_spec(tm), _tok_spec(tm), _resident((1, D_MODEL))),
                    x_lat, g1, row(norm2_g[1]), sc2, sh2, g2, *ffn, tm)
```

```python
import functools

import jax
import jax.numpy as jnp
import numpy as np
from jax import lax
from jax.experimental import pallas as pl
from jax.experimental.pallas import tpu as pltpu

F32 = jnp.float32
BF16 = jnp.bfloat16

D_MODEL = 1024
EPS = 1e-6
GRID_W = 64
WIN_H = 8
WIN_W = 16
NA_HEADS = 16
NA_HEAD_DIM = D_MODEL // NA_HEADS
HG_HEADS = 8
HG_DK = 128
D_FF = 2816
N_MOD = 6

LANES = 128
Q_ROWS = 4
Q_BLK = Q_ROWS * GRID_W
KEY_SLOTS = 3
ATTN_GROUP = 4
GROUP_W = ATTN_GROUP * NA_HEAD_DIM
BOUNDED_SPREAD = 60.0
Q_BLOCKS_PER_STEP = 8
LOG2E = 1.4426950408889634
MASKED = -1e30
SCAN_CHUNK = 64
SCAN_SUB = 8
FACTOR_RANGE = 100.0
GATE_FLOOR = 2.0 ** -100
VMEM_LIMIT = 56 * 1024 * 1024


def _dot(a, b):
    return jnp.dot(a, b, preferred_element_type=F32)


def _dot_nt(a, b):
    return lax.dot_general(a, b, (((1,), (1,)), ((), ())), preferred_element_type=F32)


def _dot_tn(a, b):
    return lax.dot_general(a, b, (((0,), (0,)), ((), ())), preferred_element_type=F32)


def _silu(x):
    half = 0.5 * x
    return half + half * jnp.tanh(half)


def _norm_mod(xf, g, sc, sh):
    ms = jnp.mean(xf * xf, axis=-1, keepdims=True)
    return (xf * lax.rsqrt(ms + EPS) * g) * (1.0 + sc) + sh


def _split_bf16(x):
    hi = x.astype(BF16)
    lo = (x - hi.astype(F32)).astype(BF16)
    return hi, lo


def _resident(shape):
    zeros = (0,) * len(shape)
    return pl.BlockSpec(shape, lambda *_: zeros, pipeline_mode=pl.Buffered(1))


def _mod_spec(arr):
    if arr.shape[0] == 1:
        return pl.BlockSpec((None, 1, D_MODEL), lambda b, i: (0, 0, 0))
    return pl.BlockSpec((None, 1, D_MODEL), lambda b, i: (b, 0, 0))


def _tok_spec(tm, width=D_MODEL):
    return pl.BlockSpec((None, tm, width), lambda b, i: (b, i, 0))


def _params(*sem, fuse=None):
    return pltpu.CompilerParams(dimension_semantics=sem, vmem_limit_bytes=VMEM_LIMIT, allow_input_fusion=fuse)


def _ada_kernel(c_ref, w_ref, b_ref, o_ref):
    s = _silu(c_ref[...]).astype(BF16)
    o_ref[...] = _dot(s, w_ref[...].astype(BF16)) + b_ref[...]


def _ada(cvec, ada_w, ada_b):
    depth, _, n = ada_w.shape
    tn = 1536
    return pl.pallas_call(
        _ada_kernel,
        grid=(depth, n // tn),
        in_specs=[pl.BlockSpec((8, D_MODEL), lambda l, j: (0, 0)),
                  pl.BlockSpec((None, D_MODEL, tn), lambda l, j: (l, 0, j)),
                  pl.BlockSpec((None, 1, tn), lambda l, j: (l, 0, j))],
        out_specs=pl.BlockSpec((None, 8, tn), lambda l, j: (l, 0, j)),
        out_shape=jax.ShapeDtypeStruct((depth, 8, n), F32),
        compiler_params=_params("arbitrary", "arbitrary"),
        name="ada",
    )(cvec, ada_w, ada_b.reshape(depth, 1, n))


def _qkv_kernel(x_ref, g_ref, sc_ref, sh_ref, w_ref, wvt_ref, gq_ref, gk_ref, red_ref, exp_ref,
                q_ref, k_ref, vt_ref):
    h = _norm_mod(x_ref[...], g_ref[...], sc_ref[...], sh_ref[...]).astype(BF16)

    def head_norm(z, gain, out_ref):
        ss = _dot((z * z).astype(BF16), red_ref[...])
        r = lax.rsqrt(ss * (1.0 / NA_HEAD_DIM) + EPS)
        rex = _dot(jnp.concatenate(_split_bf16(r), axis=1), exp_ref[...])
        zn = (z * rex * gain).astype(BF16)
        for g in range(D_MODEL // GROUP_W):
            out_ref[g] = zn[:, g * GROUP_W:(g + 1) * GROUP_W]

    q = _dot(h, w_ref[:, 0:D_MODEL])
    k = _dot(h, w_ref[:, D_MODEL:2 * D_MODEL])
    head_norm(q, gq_ref[...], q_ref)
    vt = _dot_nt(wvt_ref[...], h).astype(BF16)
    head_norm(k, gk_ref[...], k_ref)
    for j in range(vt_ref.shape[0]):
        vt_ref[j] = vt[:, j * Q_BLK:(j + 1) * Q_BLK]


def _qkv(x, g, sc, sh, w_qk, w_vt, gq, gk, red, expand, tm):
    B, T, _ = x.shape
    n_grp = D_MODEL // GROUP_W
    grp = jax.ShapeDtypeStruct((n_grp, B, T, GROUP_W), BF16)
    grp_spec = pl.BlockSpec((n_grp, None, tm, GROUP_W), lambda b, i: (0, b, i, 0))
    return pl.pallas_call(
        _qkv_kernel,
        grid=(B, T // tm),
        in_specs=[_tok_spec(tm), _resident((1, D_MODEL)), _mod_spec(sc), _mod_spec(sh),
                  _resident(w_qk.shape), _resident(w_vt.shape), _resident((1, D_MODEL)), _resident((1, D_MODEL)),
                  _resident(red.shape), _resident(expand.shape)],
        out_specs=[grp_spec, grp_spec,
                   pl.BlockSpec((None, tm // Q_BLK, D_MODEL, Q_BLK), lambda b, i: (b, i, 0, 0))],
        out_shape=[grp, grp, jax.ShapeDtypeStruct((B, T // Q_BLK, D_MODEL, Q_BLK), BF16)],
        compiler_params=_params("parallel", "parallel", fuse=[False] * 4 + [True] + [False] * 5),
        name="qkv",
    )(x, g, sc, sh, w_qk, w_vt, gq, gk, red, expand)


def _attend(tasks):
    def stage_scores(task):
        s = _scores(task)
        m = functools.reduce(jnp.maximum, [jnp.max(t, axis=0, keepdims=True) for t in s])
        return s, m

    def stage_probs(sm):
        s, m = sm
        return [jnp.exp2(t - m).astype(BF16) for t in s]

    n = len(tasks)
    scores, probs, outs = {}, {}, []
    for step in range(n + 2):
        if step < n:
            scores[step] = stage_scores(tasks[step])
        if 0 <= step - 1 < n:
            probs[step - 1] = stage_probs(scores.pop(step - 1))
        if 0 <= step - 2 < n:
            outs.append(_weighted_values(tasks[step - 2], probs.pop(step - 2)))
    return outs


def _scores(task):
    q_h, keys, _, biases = task
    s = []
    for key, bias in zip(keys, biases):
        t = _dot_nt(key(), q_h)
        s.append(t if bias is None else t + bias())
    return s


def _attend_bounded(tasks):
    def stage_probs(task):
        return [jnp.exp2(t).astype(BF16) for t in _scores(task)]

    n = len(tasks)
    probs, outs = {}, []
    for step in range(n + 1):
        if step < n:
            probs[step] = stage_probs(tasks[step])
        if step >= 1:
            outs.append(_weighted_values(tasks[step - 1], probs.pop(step - 1)))
    return outs


def _weighted_values(task, p):
    ones_rows = 16
    acc = None
    for vt, t in zip(task[2], p):
        v1 = jnp.concatenate([vt(), jnp.ones((ones_rows, t.shape[0]), BF16)], axis=0)
        part = _dot(v1, t)
        acc = part if acc is None else acc + part
    return acc[:NA_HEAD_DIM] * (1.0 / acc[NA_HEAD_DIM:NA_HEAD_DIM + 1])


def _head_queries(q_grp):
    lane_head = lax.broadcasted_iota(jnp.int32, (1, q_grp.shape[1]), 1) // NA_HEAD_DIM
    return [jnp.where(lane_head == h, q_grp, jnp.zeros_like(q_grp)) for h in range(ATTN_GROUP)]


def _head_rows(ref, h):
    return lambda: ref[h * NA_HEAD_DIM:(h + 1) * NA_HEAD_DIM, :]


def _attn_kernel(bounded_ref, q_ref, *refs):
    n_halo = Q_BLOCKS_PER_STEP + KEY_SLOTS - 1
    k_refs, v_refs = refs[:n_halo], refs[n_halo:2 * n_halo]
    kc_ref, vc_ref = refs[2 * n_halo:2 * n_halo + 2]
    bias_first, bias_mid, bias_last, o_ref = refs[2 * n_halo + 2:]
    bias_refs = [bias_first] + [bias_mid] * (Q_BLOCKS_PER_STEP - 2) + [bias_last]
    n_ctx = kc_ref.shape[0] // Q_BLK
    tasks = []
    for jb in range(Q_BLOCKS_PER_STEP):
        q_heads = _head_queries(q_ref[jb * Q_BLK:(jb + 1) * Q_BLK, :])
        for h in range(ATTN_GROUP):
            keys = ([(lambda r=r: r[...]) for r in k_refs[jb:jb + KEY_SLOTS]]
                    + [(lambda t=t: kc_ref[t * Q_BLK:(t + 1) * Q_BLK, :]) for t in range(n_ctx)])
            vts = ([_head_rows(r, h) for r in v_refs[jb:jb + KEY_SLOTS]]
                   + [_head_rows(vc_ref.at[t], h) for t in range(n_ctx)])
            biases = ([(lambda t=t, jb=jb, h=h: bias_refs[jb][h, t * Q_BLK:(t + 1) * Q_BLK, :])
                       for t in range(KEY_SLOTS)] + [None] * n_ctx)
            tasks.append((q_heads[h], keys, vts, biases))

    def finish(outs):
        for jb in range(Q_BLOCKS_PER_STEP):
            o_t = jnp.concatenate(outs[jb * ATTN_GROUP:(jb + 1) * ATTN_GROUP], axis=0)
            o_ref[jb * Q_BLK:(jb + 1) * Q_BLK, :] = o_t.T.astype(BF16)

    bounded = bounded_ref[0] > 0.5

    @pl.when(bounded)
    def _():
        finish(_attend_bounded(tasks))

    @pl.when(jnp.logical_not(bounded))
    def _():
        finish(_attend(tasks))


def _attn(bounded, q, k, vt, kc, vtc, bias):
    n_grp, B, T, gw = q.shape
    L = kc.shape[2]
    nb = T // Q_BLK
    per = Q_BLOCKS_PER_STEP
    n_halo = per + KEY_SLOTS - 1

    def blk(i, d):
        return jnp.clip(per * i + d, 0, nb - 1)

    def variant(qb):
        return jnp.where(qb == 0, 0, jnp.where(qb == nb - 1, 2, 1))

    halo = [pl.BlockSpec((None, None, Q_BLK, gw), lambda g, b, i, d=d: (g, b, blk(i, d - 1), 0))
            for d in range(n_halo)]
    halo_t = [pl.BlockSpec((None, None, gw, Q_BLK), lambda g, b, i, d=d: (b, blk(i, d - 1), g, 0))
              for d in range(n_halo)]
    biases = [pl.BlockSpec((None, ATTN_GROUP, KEY_SLOTS * Q_BLK, Q_BLK), index)
              for index in (lambda g, b, i: (variant(per * i), g, 0, 0),
                            lambda g, b, i: (1, g, 0, 0),
                            lambda g, b, i: (variant(per * i + per - 1), g, 0, 0))]
    tok = pl.BlockSpec((None, None, per * Q_BLK, gw), lambda g, b, i: (g, b, i, 0))
    return pl.pallas_call(
        _attn_kernel,
        grid=(n_grp, B, nb // per),
        in_specs=[pl.BlockSpec(memory_space=pltpu.SMEM), tok] + halo + halo_t
                 + [pl.BlockSpec((None, None, L, gw), lambda g, b, i: (g, b, 0, 0)),
                    pl.BlockSpec((None, L // Q_BLK, gw, Q_BLK), lambda g, b, i: (b, 0, g, 0))] + biases,
        out_specs=tok,
        out_shape=jax.ShapeDtypeStruct((n_grp, B, T, gw), BF16),
        compiler_params=_params("parallel", "parallel", "parallel", fuse=[False] * (4 + 2 * n_halo) + [True] * 3),
        name="attn",
    )(bounded, q, *([k] * n_halo), *([vt] * n_halo), kc, vtc, bias, bias, bias)


def _attn_ctx_kernel(q_ref, k_ref, vt_ref, o_ref):
    n_ctx = k_ref.shape[1] // Q_BLK
    tasks = []
    for g in range(q_ref.shape[0]):
        q_heads = _head_queries(q_ref[g])
        for h in range(ATTN_GROUP):
            rows = slice(g * GROUP_W + h * NA_HEAD_DIM, g * GROUP_W + (h + 1) * NA_HEAD_DIM)
            tasks.append((q_heads[h],
                          [(lambda t=t, g=g: k_ref[g, t * Q_BLK:(t + 1) * Q_BLK, :]) for t in range(n_ctx)],
                          [(lambda t=t, rows=rows: vt_ref[t, rows, :]) for t in range(n_ctx)],
                          [None] * n_ctx))
    outs = _attend(tasks)
    for g in range(q_ref.shape[0]):
        o_ref[g] = jnp.concatenate(outs[g * ATTN_GROUP:(g + 1) * ATTN_GROUP], axis=0).T.astype(BF16)


def _attn_ctx(q, k, vt):
    n_grp, B, L, gw = q.shape
    spec = pl.BlockSpec((n_grp, None, L, gw), lambda b: (0, b, 0, 0))
    return pl.pallas_call(
        _attn_ctx_kernel,
        grid=(B,),
        in_specs=[spec, spec, pl.BlockSpec((None, L // Q_BLK, D_MODEL, Q_BLK), lambda b: (b, 0, 0, 0))],
        out_specs=spec,
        out_shape=jax.ShapeDtypeStruct((n_grp, B, L, gw), BF16),
        compiler_params=_params("parallel"),
        name="attn_ctx",
    )(q, k, vt)


def _attn_bias(rpb, rows):
    n_heads = rpb.shape[0]
    n_slot = KEY_SLOTS * Q_ROWS
    cc = np.arange(GRID_W)[:, None]
    c = np.arange(GRID_W)[None, :]
    onehot = (cc - c + WIN_W - 1)[None] == np.arange(2 * WIN_W - 1)[:, None, None]
    toep = jnp.einsum('hdk,kxc->hdxc', rpb, jnp.asarray(onehot, F32), precision=lax.Precision.HIGHEST)
    col_start = np.clip(c - WIN_W // 2, 0, GRID_W - WIN_W)
    toep = jnp.where((cc >= col_start) & (cc < col_start + WIN_W), toep, MASKED)
    lo_d = WIN_H - 1 - Q_ROWS
    bias = jnp.concatenate([toep[:, lo_d - a:lo_d - a + n_slot] for a in range(Q_ROWS)], axis=-1)
    full = (n_slot, GRID_W, Q_ROWS, GRID_W)
    j = np.arange(n_slot)[:, None, None, None]
    a = np.arange(Q_ROWS)[None, None, :, None]
    kh = min(WIN_H, rows)
    lows = (np.full_like(a, Q_ROWS), a + Q_ROWS - kh // 2, np.full_like(a, 2 * Q_ROWS - kh))
    row_mask = np.stack([np.where(np.broadcast_to((j >= lo) & (j < lo + kh), full), 0.0, MASKED) for lo in lows])
    return (bias.reshape(n_heads, KEY_SLOTS * Q_BLK, Q_BLK),
            jnp.asarray(row_mask.reshape(3, KEY_SLOTS * Q_BLK, Q_BLK), F32))


MXU_DEPTH = 256
FF_BOUNDS = (0, MXU_DEPTH * (D_FF // MXU_DEPTH + 1) // 2, D_FF)


def _mix_ffn_tail(mix, x_ref, g1_ref, n2_ref, sc2_ref, sh2_ref, g2_ref, wo_ref, win_ref, wout_ref, o_ref):
    tm = x_ref.shape[0]
    n_sub = 2 if tm % (2 * Q_BLK) == 0 else 1
    halves = [slice(i * (tm // n_sub), (i + 1) * (tm // n_sub)) for i in range(n_sub)]

    def head(rows):
        x1 = x_ref[rows, :] + g1_ref[...] * mix(wo_ref, rows)
        return x1, _norm_mod(x1, n2_ref[...], sc2_ref[...], sh2_ref[...]).astype(BF16)

    def ffn(rows, x1, h2):
        acc = None
        for lo, hi in zip(FF_BOUNDS[:-1], FF_BOUNDS[1:]):
            gate = _dot(h2, win_ref[:, lo:hi])
            up = _dot(h2, win_ref[:, D_FF + lo:D_FF + hi])
            t = _dot((_silu(gate) * up).astype(BF16), wout_ref[lo:hi, :])
            acc = t if acc is None else acc + t
        o_ref[rows, :] = x1 + g2_ref[...] * acc

    heads = [head(rows) for rows in halves]
    for rows, (x1, h2) in zip(halves, heads):
        ffn(rows, x1, h2)


def _na_mix_ffn_kernel(a_ref, *rest):
    def mix(wo_ref, rows):
        parts = [_dot(a_ref[g, rows, :], wo_ref[g * GROUP_W:(g + 1) * GROUP_W, :]) for g in range(a_ref.shape[0])]
        return functools.reduce(jnp.add, parts)

    _mix_ffn_tail(mix, *rest)


def _hg_mix_ffn_kernel(of_ref, ob_ref, sg_ref, gn_ref, *rest):
    def mix(wo_ref, rows):
        o = of_ref[rows, :].astype(F32) + ob_ref[rows, :].astype(F32)
        parts = []
        for h in range(HG_HEADS):
            oh = o[:, h * HG_DK:(h + 1) * HG_DK]
            ms = jnp.mean(oh * oh, axis=-1, keepdims=True)
            parts.append(oh * lax.rsqrt(ms + EPS))
        a = (jnp.concatenate(parts, axis=1) * gn_ref[...] * sg_ref[rows, :].astype(F32)).astype(BF16)
        return _dot(a, wo_ref[...])

    _mix_ffn_tail(mix, *rest)


def _mix_ffn(kernel, name, lead, lead_specs, x, g1, n2, sc2, sh2, g2, wo, win, wout, tm):
    B, T, _ = x.shape
    return pl.pallas_call(
        kernel,
        grid=(B, T // tm),
        in_specs=list(lead_specs) + [
            _tok_spec(tm), _mod_spec(g1), _resident((1, D_MODEL)), _mod_spec(sc2), _mod_spec(sh2), _mod_spec(g2),
            _resident(wo.shape), _resident(win.shape), _resident(wout.shape)],
        out_specs=_tok_spec(tm),
        out_shape=jax.ShapeDtypeStruct((B, T, D_MODEL), F32),
        compiler_params=_params("parallel", "parallel"),
        name=name,
    )(*lead, x, g1, n2, sc2, sh2, g2, wo, win, wout)


def _hg_proj_kernel(x_ref, g_ref, sc_ref, sh_ref, w_ref, lb_ref, tri_split_ref, q_ref, v_ref, sg_ref, kk_ref, cum_ref):
    halves = [slice(0, x_ref.shape[0])]
    hs = [_norm_mod(x_ref[rows, :], g_ref[...], sc_ref[...], sh_ref[...]).astype(BF16) for rows in halves]

    def gates(d, rows, f_pre):
        lb = lb_ref[d]
        half = 0.5 * (1.0 - lb)
        th = jnp.tanh(0.5 * f_pre)
        f = jnp.maximum(0.5 * (1.0 + lb) + half * th, GATE_FLOOR)
        kk_ref[d, rows, :] = (half - half * th).astype(BF16)
        hi, lo = _split_bf16(jnp.log2(f))
        for c in range(f_pre.shape[0] // SCAN_CHUNK):
            sub = slice(c * SCAN_CHUNK, (c + 1) * SCAN_CHUNK)
            parts = jnp.concatenate([hi[sub], lo[sub]], axis=0)
            cum_ref[d, rows.start + sub.start:rows.start + sub.stop, :] = _dot(tri_split_ref[d], parts)

    def store(ref, fn, rows, z):
        ref[rows, :] = fn(z).astype(BF16)

    work = []
    for rows, h in zip(halves, hs):
        work += [(h, 3, functools.partial(gates, 0, rows)),
                 (h, 4, functools.partial(gates, 1, rows)),
                 (h, 0, functools.partial(store, q_ref, _silu, rows)),
                 (h, 2, functools.partial(store, sg_ref, _silu, rows)),
                 (h, 1, functools.partial(store, v_ref, lambda z: z, rows))]

    def project(item):
        h, n, _ = item
        return _dot(h, w_ref[:, n * D_MODEL:(n + 1) * D_MODEL])

    pending = project(work[0])
    for n, item in enumerate(work):
        z = pending
        if n + 1 < len(work):
            pending = project(work[n + 1])
        item[2](z)


def _hg_proj(x, g, sc, sh, w, lb, tri_split, tm):
    B, T, _ = x.shape
    tok = jax.ShapeDtypeStruct((B, T, D_MODEL), BF16)
    two = pl.BlockSpec((2, None, tm, D_MODEL), lambda b, i: (0, b, i, 0))
    return pl.pallas_call(
        _hg_proj_kernel,
        grid=(B, T // tm),
        in_specs=[_tok_spec(tm), _resident((1, D_MODEL)), _mod_spec(sc), _mod_spec(sh),
                  _resident(w.shape), _resident(lb.shape), _resident(tri_split.shape)],
        out_specs=[_tok_spec(tm)] * 3 + [two, two],
        out_shape=[tok] * 3 + [jax.ShapeDtypeStruct((2, B, T, D_MODEL), BF16),
                               jax.ShapeDtypeStruct((2, B, T, D_MODEL), F32)],
        compiler_params=_params("parallel", "parallel"),
        name="hg_proj",
    )(x, g, sc, sh, w, lb, tri_split)


def _hg_scan_kernel(qf_ref, vf_ref, kkf_ref, cumf_ref, qb_ref, vb_ref, kkb_ref, cumb_ref, tri_ref, s0_ref,
                    of_ref, ob_ref, st_ref, a_scr):
    @pl.when(pl.program_id(1) == 0)
    def _():
        st_ref[...] = s0_ref[...]

    C = SCAN_CHUNK
    n_sub = qf_ref.shape[0] // C
    dirs = ((qf_ref, vf_ref, kkf_ref, cumf_ref, of_ref), (qb_ref, vb_ref, kkb_ref, cumb_ref, ob_ref))
    streams = [(d, slice((j if d == 0 else n_sub - 1 - j) * C, (j if d == 0 else n_sub - 1 - j) * C + C))
               for j in range(n_sub) for d in range(2)]
    pair_w = 2 * HG_DK
    pairs = [slice(p * pair_w, (p + 1) * pair_w) for p in range(HG_HEADS // 2)]
    first = lax.broadcasted_iota(jnp.int32, (1, pair_w), 1) < HG_DK

    def block_diag(x):
        zero = jnp.zeros_like(x)
        return jnp.concatenate([jnp.where(first, x, zero), jnp.where(first, zero, x)], axis=0)

    def chunk_total(d, rows):
        last = rows.stop - 1 if d == 0 else rows.start
        return dirs[d][3][last:last + 1, :]

    out_of_range = [jnp.max(-0.5 * chunk_total(d, rows)) > FACTOR_RANGE for d, rows in streams]

    def stage_prep(n):
        d, rows = streams[n]
        q_ref, _, kk_ref, cum_ref, _ = dirs[d]
        b = cum_ref[rows, :]
        tot = chunk_total(d, rows)
        mid = 0.5 * tot
        q = q_ref[rows, :]
        kk = kk_ref[rows, :]
        return dict(q_in=q * jnp.exp2(b - mid).astype(BF16), k_in=kk * jnp.exp2(mid - b).astype(BF16),
                    q_st=q * jnp.exp2(b).astype(BF16), k_st=kk * jnp.exp2(tot - b).astype(BF16),
                    decay=jnp.exp2(tot))

    def stage_intra(n, p, check_range):
        d, rows = streams[n]
        seen = jnp.concatenate([tri_ref[d]] * 2, axis=1) > 0.0
        a_s = [jnp.where(seen, _dot_nt(p["q_in"][:, pr], block_diag(p["k_in"][:, pr])), 0.0).astype(BF16)
               for pr in pairs]
        if not check_range:
            return a_s
        a_scr[n] = jnp.concatenate(a_s, axis=1)

        @pl.when(out_of_range[n])
        def _():
            a_scr[n] = _pairwise_intra(dirs[d][0][rows, :].astype(F32), dirs[d][2][rows, :].astype(F32),
                                       dirs[d][3], rows, d == 0).astype(BF16)

        return [a_scr[n, :, i * 2 * C:(i + 1) * 2 * C] for i in range(len(pairs))]

    def stage_values(n, p, a_s):
        d, rows = streams[n]
        v_ref = dirs[d][1]
        o_intra = [_dot(a, block_diag(v_ref[rows, pr])) for a, pr in zip(a_s, pairs)]
        update = [_dot_tn(v_ref[rows, h * HG_DK:(h + 1) * HG_DK], p["k_st"][:, h * HG_DK:(h + 1) * HG_DK])
                  for h in range(HG_HEADS)]
        return o_intra, update

    def stage_state(n, p, o_intra, update):
        d, rows = streams[n]
        zero = jnp.zeros((HG_DK, HG_DK), BF16)
        outs = []
        for i, pr in enumerate(pairs):
            st_a, st_b = st_ref[d, 2 * i], st_ref[d, 2 * i + 1]
            st_bd = jnp.concatenate([jnp.concatenate([st_a.astype(BF16), zero], axis=1),
                                     jnp.concatenate([zero, st_b.astype(BF16)], axis=1)], axis=0)
            outs.append(o_intra[i] + _dot_nt(p["q_st"][:, pr], st_bd))
            st_ref[d, 2 * i] = st_a * p["decay"][:, pr][:, :HG_DK] + update[2 * i]
            st_ref[d, 2 * i + 1] = st_b * p["decay"][:, pr][:, HG_DK:] + update[2 * i + 1]
        dirs[d][4][rows, :] = jnp.concatenate(outs, axis=1).astype(BF16)

    def run(check_range):
        n_str = len(streams)
        prep, intra, vals = {}, {}, {}
        for t in range(n_str + 3):
            if t < n_str:
                prep[t] = stage_prep(t)
            if 0 <= t - 1 < n_str:
                intra[t - 1] = stage_intra(t - 1, prep[t - 1], check_range)
            if 0 <= t - 2 < n_str:
                vals[t - 2] = stage_values(t - 2, prep[t - 2], intra.pop(t - 2))
            if 0 <= t - 3 < n_str:
                stage_state(t - 3, prep.pop(t - 3), *vals.pop(t - 3))

    any_out = functools.reduce(jnp.logical_or, out_of_range)
    pl.when(jnp.logical_not(any_out))(lambda: run(False))
    pl.when(any_out)(lambda: run(True))


def _pairwise_intra(q, kk, cum_ref, rows, forward):
    C = q.shape[0]
    b = cum_ref[rows, :]
    t_idx = lax.broadcasted_iota(jnp.int32, (C, 1), 0)
    lane = lax.broadcasted_iota(jnp.int32, (1, HG_HEADS * C), 1)

    def body(s, acc):
        b_s = cum_ref[pl.ds(rows.start + s, 1), :]
        k_s = jnp.sum(jnp.where(t_idx == s, kk, 0.0), axis=0, keepdims=True)
        w = q * (k_s * jnp.exp2(jnp.minimum(b - b_s, 0.0)))
        seen = (t_idx >= s) if forward else (t_idx <= s)
        for h in range(HG_HEADS):
            col = jnp.sum(w[:, h * HG_DK:(h + 1) * HG_DK], axis=1, keepdims=True)
            acc = jnp.where(lane == h * C + s, jnp.where(seen, col, 0.0), acc)
        return acc

    return lax.fori_loop(0, C, body, jnp.zeros((C, HG_HEADS * C), F32))


def _hg_scan(q, v, kk, cum, tri, s0):
    B, T, _ = q.shape
    C = SCAN_CHUNK * max(s for s in range(1, SCAN_SUB + 1) if T % (SCAN_CHUNK * s) == 0)
    nc = T // C
    fwd = pl.BlockSpec((None, C, D_MODEL), lambda b, i: (b, i, 0))
    bwd = pl.BlockSpec((None, C, D_MODEL), lambda b, i: (b, nc - 1 - i, 0))
    dfwd = pl.BlockSpec((None, None, C, D_MODEL), lambda b, i: (0, b, i, 0))
    dbwd = pl.BlockSpec((None, None, C, D_MODEL), lambda b, i: (1, b, nc - 1 - i, 0))
    state = pl.BlockSpec((2, None, HG_HEADS, HG_DK, HG_DK), lambda b, i: (0, b, 0, 0, 0))
    tok = jax.ShapeDtypeStruct((B, T, D_MODEL), BF16)
    return pl.pallas_call(
        _hg_scan_kernel,
        grid=(B, nc),
        in_specs=[fwd, fwd, dfwd, dfwd, bwd, bwd, dbwd, dbwd, _resident(tri.shape), state],
        out_specs=[fwd, bwd, state],
        out_shape=[tok, tok, jax.ShapeDtypeStruct((2, B, HG_HEADS, HG_DK, HG_DK), F32)],
        scratch_shapes=[pltpu.VMEM((2 * C // SCAN_CHUNK, SCAN_CHUNK, HG_HEADS * SCAN_CHUNK), BF16)],
        compiler_params=_params("parallel", "arbitrary"),
        name="hg_scan",
    )(q, v, kk, cum, q, v, kk, cum, tri, s0)


def _tile(n, pref):
    return pref if n % pref == 0 else n


def kernel(x, c, ctx, c_ctx, ada_w, ada_b, norm1_g, norm2_g, na_w_qkv, na_w_o, na_q_gain, na_k_gain, na_rpb,
           hg_w_in, hg_lower, hg_norm_g, hg_w_o, ffn_w_in, ffn_w_out):
    B, T, _ = x.shape
    L = ctx.shape[1]
    rows = T // GRID_W
    assert B + 1 <= 8 and T % (Q_BLOCKS_PER_STEP * Q_BLK) == 0 and rows >= WIN_H
    assert T % SCAN_CHUNK == 0 and L % Q_BLK == 0
    tm = _tile(T, 512)
    tl = _tile(B * L, 512)
    row = lambda a: a.reshape(1, -1).astype(F32)

    cvec = jnp.zeros((8, D_MODEL), F32).at[:B].set(c).at[B].set(c_ctx)
    mods = _ada(cvec, ada_w, ada_b)

    def mod_rows(i):
        lat = [m.reshape(B, 1, D_MODEL) for m in jnp.split(mods[i, :B], N_MOD, axis=-1)]
        cx = [m.reshape(1, 1, D_MODEL) for m in jnp.split(mods[i, B], N_MOD, axis=-1)]
        return lat, cx

    (sh1, sc1, g1, sh2, sc2, g2), (csh1, csc1, cg1, csh2, csc2, cg2) = mod_rows(0)
    w_qk = na_w_qkv[0, :, :2 * D_MODEL].astype(BF16)
    w_vt = na_w_qkv[0, :, 2 * D_MODEL:].T.astype(BF16)
    gq = row(jnp.tile(na_q_gain[0], NA_HEADS) * (NA_HEAD_DIM ** -0.5 * LOG2E))
    gk = row(jnp.tile(na_k_gain[0], NA_HEADS))
    head_of = np.arange(D_MODEL) // NA_HEAD_DIM
    red = jnp.asarray(head_of[:, None] == np.arange(LANES)[None, :], BF16)
    expand = jnp.asarray(np.arange(2 * LANES)[:, None] % LANES == head_of[None, :], BF16)
    n1 = row(norm1_g[0])
    q_l, k_l, vt_l = _qkv(x, n1, sc1, sh1, w_qk, w_vt, gq, gk, red, expand, _tile(T, 1024))
    ctx_flat = ctx.reshape(1, B * L, D_MODEL)
    n_grp = D_MODEL // GROUP_W
    q_c, k_c, vt_c = _qkv(ctx_flat, n1, csc1, csh1, w_qk, w_vt, gq, gk, red, expand, tl)
    q_c, k_c = q_c.reshape(n_grp, B, L, GROUP_W), k_c.reshape(n_grp, B, L, GROUP_W)
    vt_c = vt_c.reshape(B, L // Q_BLK, D_MODEL, Q_BLK)
    rpb2 = na_rpb[0].astype(F32) * LOG2E
    qk_max = NA_HEAD_DIM * jnp.max(jnp.abs(gq)) * jnp.max(jnp.abs(gk))
    bound = qk_max + jnp.maximum(jnp.max(rpb2), 0.0)
    bounded = (bound < BOUNDED_SPREAD).astype(F32).reshape(1)
    bias, row_mask = _attn_bias(rpb2, rows)
    a_l = _attn(bounded, q_l, k_l, vt_l, k_c, vt_c, bias[None] + row_mask[:, None])
    a_c = _attn_ctx(q_c, k_c, vt_c)
    ffn = (na_w_o[0].astype(BF16), ffn_w_in[0].astype(BF16), ffn_w_out[0].astype(BF16))
    n2 = row(norm2_g[0])
    grp_spec = lambda t: pl.BlockSpec((D_MODEL // GROUP_W, None, t, GROUP_W), lambda b, i: (0, b, i, 0))
    x_lat = _mix_ffn(_na_mix_ffn_kernel, "na_mix_ffn", (a_l,), (grp_spec(tm),),
                     x, g1, n2, sc2, sh2, g2, *ffn, tm)
    x_ctx = _mix_ffn(_na_mix_ffn_kernel, "na_mix_ffn_ctx", (a_c.reshape(n_grp, 1, B * L, GROUP_W),), (grp_spec(tl),),
                     ctx_flat, cg1, n2, csc2, csh2, cg2, *ffn, tl)

    (sh1, sc1, g1, sh2, sc2, g2), (csh1, csc1, _, _, _, _) = mod_rows(1)
    lbs = jnp.cumsum(jax.nn.softmax(hg_lower.astype(F32), axis=0), axis=0)
    lb = (lbs - lbs[:1])[1].reshape(2, 1, D_MODEL)
    w_in = hg_w_in[0].astype(BF16)
    n1 = row(norm1_g[1])
    t_idx = np.arange(SCAN_CHUNK)
    tri_np = np.stack([t_idx[None, :] <= t_idx[:, None], t_idx[None, :] >= t_idx[:, None]])
    tri = jnp.asarray(tri_np, F32)
    tri_split = jnp.asarray(np.tile(tri_np, (1, 1, 2)), BF16)
    q_c, v_c, _, kk_c, cum_c = _hg_proj(x_ctx, n1, csc1, csh1, w_in, lb, tri_split, tl)
    q_c, v_c = q_c.reshape(B, L, D_MODEL), v_c.reshape(B, L, D_MODEL)
    kk_c, cum_c = kk_c.reshape(2, B, L, D_MODEL), cum_c.reshape(2, B, L, D_MODEL)
    q_l, v_l, sg_l, kk_l, cum_l = _hg_proj(x_lat, n1, sc1, sh1, w_in, lb, tri_split, tm)
    zero = jnp.zeros((2, B, HG_HEADS, HG_DK, HG_DK), F32)
    _, _, s_ctx = _hg_scan(q_c, v_c, kk_c, cum_c, tri, zero)
    o_f, o_b, _ = _hg_scan(q_l, v_l, kk_l, cum_l, tri, s_ctx)
    ffn = (hg_w_o[0].astype(BF16), ffn_w_in[1].astype(BF16), ffn_w_out[1].astype(BF16))
    gn = row(jnp.tile(hg_norm_g[0], HG_HEADS))
    return _mix_ffn(_hg_mix_ffn_kernel, "hg_mix_ffn", (o_f, o_b, sg_l, gn),
                    (_tok_spec(tm), _tok_spec(tm), _tok_spec(tm), _resident((1, D_MODEL))),
                    x_lat, g1, row(norm2_g[1]), sc2, sh2, g2, *ffn, tm)
```
